```python
import jax, jax.numpy as jnp
from jax import lax
import numpy as np

D_MODEL = 4096
BATCH = 1
SEQ = 8192
DEPTH = 1

NORM_EPS = 1e-6
NEG_INF = -1e30

HEAD_DIM = 128
ROPE_THETA = 10000.0
MAX_START = 4096

DILATED_GROUPS = ((128, 1), (512, 4), (2048, 16))
HEADS_PER_GROUP = 4
A_HEADS = HEADS_PER_GROUP * len(DILATED_GROUPS)
A_WIDTH = A_HEADS * HEAD_DIM
A_OUT_WIDTH = HEADS_PER_GROUP * HEAD_DIM
ATTN_BLOCK = 128

DN_HEADS = 16
DN_DK = 128
DN_DV = 128
DN_CONV = 4
DN_CHUNK = 64
DN_K_WIDTH = DN_HEADS * DN_DK
DN_V_WIDTH = DN_HEADS * DN_DV
DN_QKV_WIDTH = 2 * DN_K_WIDTH + DN_V_WIDTH

IN_SPLITS = (A_WIDTH, A_WIDTH, A_WIDTH, DN_QKV_WIDTH, DN_V_WIDTH, DN_HEADS, DN_HEADS, D_MODEL, D_MODEL)
IN_WIDTH = 3 * A_WIDTH + DN_QKV_WIDTH + DN_V_WIDTH + 2 * DN_HEADS + 2 * D_MODEL

PEER_HEADS = 8
PEER_KEYS = 128
PEER_EXPERTS = PEER_KEYS * PEER_KEYS
PEER_QDIM = 256
PEER_TOPK = 16
PEER_BLOCK = 64

kernel_name = "hybrid_dilated_attn_gdn_peer_block"


def rms_norm(x, w):
    xf = x.astype(jnp.float32)
    y = xf * lax.rsqrt(jnp.mean(xf * xf, axis=-1, keepdims=True) + NORM_EPS)
    return (y * w.astype(jnp.float32)).astype(x.dtype)


def l2_norm(x):
    return x * lax.rsqrt(jnp.sum(x * x, axis=-1, keepdims=True) + NORM_EPS)


def modulate(x, shift, scale):
    return x * (1.0 + scale[:, None, :]) + shift[:, None, :]


def rotary(x, positions):
    half = x.shape[-1] // 2
    inv_freq = ROPE_THETA ** (-jnp.arange(half, dtype=jnp.float32) / half)
    ang = positions.astype(jnp.float32)[..., None] * inv_freq
    cos = jnp.cos(ang)[:, :, None, :]
    sin = jnp.sin(ang)[:, :, None, :]
    xf = x.astype(jnp.float32)
    x1, x2 = xf[..., :half], xf[..., half:]
    return jnp.concatenate([x1 * cos - x2 * sin, x2 * cos + x1 * sin], axis=-1).astype(x.dtype)


def dilated_window_attention(q, k, v, window, dilation):
    B, S, H, Dh = q.shape
    L = S // dilation
    w_sub = window // dilation
    nb = -(-L // ATTN_BLOCK)
    Lp = nb * ATTN_BLOCK

    def gather_stride(a):
        a = a.astype(jnp.float32).reshape(B, L, dilation, H, Dh).transpose(0, 2, 1, 3, 4)
        a = jnp.pad(a, ((0, 0), (0, 0), (0, Lp - L), (0, 0), (0, 0)))
        return a.reshape(B, dilation, nb, ATTN_BLOCK, H, Dh)

    def with_prev(a):
        prev = jnp.pad(a, ((0, 0), (0, 0), (1, 0), (0, 0), (0, 0), (0, 0)))[:, :, :-1]
        return jnp.concatenate([prev, a], axis=3)

    qb = gather_stride(q)
    kk = with_prev(gather_stride(k))
    vv = with_prev(gather_stride(v))
    s = jnp.einsum('brnqhe,brnkhe->brnhqk', qb, kk) * (Dh ** -0.5)
    qi = jnp.arange(ATTN_BLOCK)[:, None]
    kj = jnp.arange(2 * ATTN_BLOCK)[None, :] - ATTN_BLOCK
    dist = qi - kj
    blk = jnp.arange(nb)[:, None, None]
    valid = (dist >= 0) & (dist <= w_sub) & (blk * ATTN_BLOCK + kj >= 0)
    s = jnp.where(valid[:, None, :, :], s, NEG_INF)
    m = jnp.max(s, axis=-1, keepdims=True)
    p = jnp.exp(s - m)
    den = jnp.sum(p, axis=-1)
    den_t = den.transpose(0, 1, 2, 4, 3)
    o = jnp.einsum('brnhqk,brnkhe->brnqhe', p, vv) / den_t[..., None]
    lse = m[..., 0].transpose(0, 1, 2, 4, 3) + jnp.log(den_t)

    def ungather(a):
        a = a.reshape(B, dilation, Lp, *a.shape[4:])[:, :, :L]
        a = jnp.moveaxis(a, 1, 2)
        return a.reshape(B, S, *a.shape[3:])

    return ungather(o), ungather(lse)


def dilated_attention_mixer(q, k, v, positions):
    B, S = q.shape[:2]
    q = rotary(q, positions)
    k = rotary(k, positions)
    outs, lses = [], []
    for g, (window, dilation) in enumerate(DILATED_GROUPS):
        sl = slice(g * HEADS_PER_GROUP, (g + 1) * HEADS_PER_GROUP)
        o, lse = dilated_window_attention(q[:, :, sl], k[:, :, sl], v[:, :, sl], window, dilation)
        outs.append(o)
        lses.append(lse)
    w = jax.nn.softmax(jnp.stack(lses, axis=0), axis=0)
    o = jnp.sum(jnp.stack(outs, axis=0) * w[..., None], axis=0)
    return o.reshape(B, S, A_OUT_WIDTH).astype(q.dtype)


def causal_depthwise_conv(x, w):
    K, C = w.shape
    y = lax.conv_general_dilated(x, w[:, None, :], window_strides=(1,), padding=((K - 1, 0),),
                                 dimension_numbers=('NWC', 'WIO', 'NWC'), feature_group_count=C)
    return jax.nn.silu(y)


def chunk_gated_delta_rule(q, k, v, g, beta):
    B, S, H, Dk = q.shape
    Dv = v.shape[-1]
    n = S // DN_CHUNK
    q = q * (Dk ** -0.5)

    def to_chunks(a):
        return a.reshape(B, n, DN_CHUNK, H, *a.shape[3:]).swapaxes(2, 3)

    qc, kc, vc, gc, bc = map(to_chunks, (q, k, v, g, beta))
    gcum = jnp.cumsum(gc, axis=-1)
    idx = jnp.arange(DN_CHUNK)
    lower_incl = idx[:, None] >= idx[None, :]
    strict = idx[:, None] > idx[None, :]
    decay = jnp.exp(jnp.where(lower_incl, gcum[..., :, None] - gcum[..., None, :], -jnp.inf))
    kb = kc * bc[..., None]
    vb = vc * bc[..., None]
    Lmat = jnp.where(strict, jnp.einsum('bnhid,bnhjd->bnhij', kb, kc) * decay, 0.0)
    eye = jnp.eye(DN_CHUNK, dtype=jnp.float32)
    T = lax.linalg.triangular_solve(eye + Lmat, jnp.broadcast_to(eye, Lmat.shape),
                                    left_side=True, lower=True)
    u = jnp.einsum('bnhij,bnhjd->bnhid', T, vb)
    w = jnp.einsum('bnhij,bnhjd->bnhid', T, kb * jnp.exp(gcum)[..., None])
    a_intra = jnp.where(lower_incl, jnp.einsum('bnhid,bnhjd->bnhij', qc, kc) * decay, 0.0)

    def step(state, inp):
        q_i, k_i, u_i, w_i, a_i, g_i = inp
        v_new = u_i - jnp.einsum('bhcd,bhde->bhce', w_i, state)
        o = (jnp.einsum('bhcd,bhde->bhce', q_i * jnp.exp(g_i)[..., None], state)
             + jnp.einsum('bhij,bhje->bhie', a_i, v_new))
        g_last = g_i[..., -1]
        k_dec = k_i * jnp.exp(g_last[..., None] - g_i)[..., None]
        state = state * jnp.exp(g_last)[..., None, None] + jnp.einsum('bhcd,bhce->bhde', k_dec, v_new)
        return state, o

    xs = tuple(jnp.moveaxis(a, 1, 0) for a in (qc, kc, u, w, a_intra, gcum))
    state0 = jnp.zeros((B, H, Dk, Dv), jnp.float32)
    _, o = lax.scan(step, state0, xs)
    return o.transpose(1, 0, 3, 2, 4).reshape(B, S, H, Dv)


def gated_deltanet_mixer(qkv, z, beta_raw, a_raw, conv_w, a_log, dt_bias, o_norm_w):
    B, S, _ = qkv.shape
    qkv = causal_depthwise_conv(qkv, conv_w)
    q, k, v = jnp.split(qkv, [DN_K_WIDTH, 2 * DN_K_WIDTH], axis=-1)
    q = l2_norm(q.reshape(B, S, DN_HEADS, DN_DK).astype(jnp.float32))
    k = l2_norm(k.reshape(B, S, DN_HEADS, DN_DK).astype(jnp.float32))
    v = v.reshape(B, S, DN_HEADS, DN_DV).astype(jnp.float32)
    beta = jax.nn.sigmoid(beta_raw.astype(jnp.float32))
    g = -jnp.exp(a_log.astype(jnp.float32)) * jax.nn.softplus(a_raw.astype(jnp.float32) + dt_bias.astype(jnp.float32))
    o = chunk_gated_delta_rule(q, k, v, g, beta)
    o = rms_norm(o, o_norm_w) * jax.nn.silu(z.reshape(B, S, DN_HEADS, DN_DV).astype(jnp.float32))
    return o.reshape(B, S, DN_V_WIDTH).astype(qkv.dtype)


def peer_ffn(h, w_q, keys_1, keys_2, expert_u, expert_v):
    B, S, D = h.shape
    half = PEER_QDIM // 2
    q = (h @ w_q).astype(jnp.float32).reshape(B, S, PEER_HEADS, 2, half)
    s1 = jnp.einsum('bshd,nd->bshn', q[..., 0, :], keys_1.astype(jnp.float32))
    s2 = jnp.einsum('bshd,nd->bshn', q[..., 1, :], keys_2.astype(jnp.float32))
    v1, i1 = lax.top_k(s1, PEER_TOPK)
    v2, i2 = lax.top_k(s2, PEER_TOPK)
    cand_s = (v1[..., :, None] + v2[..., None, :]).reshape(B, S, PEER_HEADS, PEER_TOPK * PEER_TOPK)
    cand_i = (i1[..., :, None] * PEER_KEYS + i2[..., None, :]).reshape(B, S, PEER_HEADS, PEER_TOPK * PEER_TOPK)
    top_s, top_pos = lax.top_k(cand_s, PEER_TOPK)
    idx = jnp.take_along_axis(cand_i, top_pos, axis=-1)
    gates = jax.nn.softmax(top_s, axis=-1)
    n_tok = B * S
    K = PEER_HEADS * PEER_TOPK
    nblk = n_tok // PEER_BLOCK
    hb = h.reshape(nblk, PEER_BLOCK, D)
    ib = idx.reshape(nblk, PEER_BLOCK, K)
    gb = gates.astype(h.dtype).reshape(nblk, PEER_BLOCK, K)

    def retrieve(args):
        h_t, i_t, g_t = args
        u = jnp.take(expert_u, i_t, axis=0)
        act = g_t * jax.nn.gelu(jnp.einsum('td,tkd->tk', h_t, u), approximate=False)
        v = jnp.take(expert_v, i_t, axis=0)
        return jnp.einsum('tk,tkd->td', act, v)

    y = lax.map(retrieve, (hb, ib, gb))
    return y.reshape(B, S, D)


def setup_inputs(seed: int = 0) -> dict:
    key = jax.random.key(seed)
    ks = jax.random.split(key, 24)

    def nrm(k, shape, scale):
        return jax.random.normal(k, shape, jnp.float32) * scale

    x = nrm(ks[0], (BATCH, SEQ, D_MODEL), 1.0)
    c = nrm(ks[1], (BATCH, D_MODEL), 1.0)
    start = jax.random.randint(ks[2], (BATCH, 1), 0, MAX_START, dtype=jnp.int32)
    positions = start + jnp.arange(SEQ, dtype=jnp.int32)[None, :]
    w_ada = nrm(ks[3], (DEPTH, D_MODEL, 6 * D_MODEL), 0.5 * D_MODEL ** -0.5)
    b_ada = nrm(ks[4], (DEPTH, 6 * D_MODEL), 0.02)
    attn_pre_norm = 1.0 + nrm(ks[5], (DEPTH, D_MODEL), 0.1)
    attn_post_norm = 1.0 + nrm(ks[6], (DEPTH, D_MODEL), 0.1)
    w_in = nrm(ks[7], (DEPTH, D_MODEL, IN_WIDTH), D_MODEL ** -0.5)
    conv_w = nrm(ks[8], (DEPTH, DN_CONV, DN_QKV_WIDTH), DN_CONV ** -0.5)
    a_log = jnp.log(jax.random.uniform(ks[9], (DEPTH, DN_HEADS), jnp.float32, 1.0, 16.0))
    dt = jnp.exp(jax.random.uniform(ks[10], (DEPTH, DN_HEADS), jnp.float32, np.log(1e-3), np.log(1e-1)))
    dt_bias = dt + jnp.log(-jnp.expm1(-dt))
    dn_norm_w = 1.0 + nrm(ks[11], (DEPTH, DN_DV), 0.1)
    w_o_attn = nrm(ks[12], (DEPTH, A_OUT_WIDTH, D_MODEL), A_OUT_WIDTH ** -0.5)
    w_o_dn = nrm(ks[13], (DEPTH, DN_V_WIDTH, D_MODEL), DN_V_WIDTH ** -0.5)
    w_out = nrm(ks[14], (DEPTH, D_MODEL, D_MODEL), D_MODEL ** -0.5)
    ffn_pre_norm = 1.0 + nrm(ks[15], (DEPTH, D_MODEL), 0.1)
    ffn_post_norm = 1.0 + nrm(ks[16], (DEPTH, D_MODEL), 0.1)
    w_peer_q = nrm(ks[17], (DEPTH, D_MODEL, PEER_HEADS * PEER_QDIM), D_MODEL ** -0.5)
    peer_keys_1 = nrm(ks[18], (DEPTH, PEER_KEYS, PEER_QDIM // 2), (PEER_QDIM // 2) ** -0.5)
    peer_keys_2 = nrm(ks[19], (DEPTH, PEER_KEYS, PEER_QDIM // 2), (PEER_QDIM // 2) ** -0.5)
    expert_u = nrm(ks[20], (DEPTH, PEER_EXPERTS, D_MODEL), D_MODEL ** -0.5)
    expert_v = nrm(ks[21], (DEPTH, PEER_EXPERTS, D_MODEL), 1.0)
    return {"x": x, "c": c, "positions": positions, "w_ada": w_ada, "b_ada": b_ada,
            "attn_pre_norm": attn_pre_norm, "attn_post_norm": attn_post_norm, "w_in": w_in,
            "conv_w": conv_w, "a_log": a_log, "dt_bias": dt_bias, "dn_norm_w": dn_norm_w,
            "w_o_attn": w_o_attn, "w_o_dn": w_o_dn, "w_out": w_out,
            "ffn_pre_norm": ffn_pre_norm, "ffn_post_norm": ffn_post_norm, "w_peer_q": w_peer_q,
            "peer_keys_1": peer_keys_1, "peer_keys_2": peer_keys_2,
            "expert_u": expert_u, "expert_v": expert_v}


def reference(x, c, positions, w_ada, b_ada, attn_pre_norm, attn_post_norm, w_in, conv_w, a_log,
              dt_bias, dn_norm_w, w_o_attn, w_o_dn, w_out, ffn_pre_norm, ffn_post_norm, w_peer_q,
              peer_keys_1, peer_keys_2, expert_u, expert_v):
    B, S, D = x.shape
    offsets = [int(o) for o in np.cumsum(IN_SPLITS)[:-1]]
    c_act = jax.nn.silu(c)
    for l in range(DEPTH):
        mod = c_act @ w_ada[l] + b_ada[l]
        shift_mix, scale_mix, gate_mix, shift_ffn, scale_ffn, gate_ffn = jnp.split(mod, 6, axis=-1)

        h = modulate(rms_norm(x, attn_pre_norm[l]), shift_mix, scale_mix)
        proj = h @ w_in[l]
        q_a, k_a, v_a, qkv_d, z_d, beta_raw, a_raw, branch_gate_a, branch_gate_d = jnp.split(proj, offsets, axis=-1)
        y_a = dilated_attention_mixer(q_a.reshape(B, S, A_HEADS, HEAD_DIM),
                                      k_a.reshape(B, S, A_HEADS, HEAD_DIM),
                                      v_a.reshape(B, S, A_HEADS, HEAD_DIM), positions)
        y_d = gated_deltanet_mixer(qkv_d, z_d, beta_raw, a_raw, conv_w[l], a_log[l], dt_bias[l], dn_norm_w[l])
        merged = (jax.nn.sigmoid(branch_gate_a) * (y_a @ w_o_attn[l])
                  + jax.nn.sigmoid(branch_gate_d) * (y_d @ w_o_dn[l]))
        mix_out = merged @ w_out[l]
        x = x + gate_mix[:, None, :] * rms_norm(mix_out, attn_post_norm[l])

        h = modulate(rms_norm(x, ffn_pre_norm[l]), shift_ffn, scale_ffn)
        ffn_out = peer_ffn(h, w_peer_q[l], peer_keys_1[l], peer_keys_2[l], expert_u[l], expert_v[l])
        x = x + gate_ffn[:, None, :] * rms_norm(ffn_out, ffn_post_norm[l])
    return x
```

```python
import functools

import jax
import jax.numpy as jnp
from jax import lax
from jax.experimental import pallas as pl
from jax.experimental.pallas import tpu as pltpu

D_MODEL = 4096
NORM_EPS = 1e-6
NEG_INF = -1e30

HEAD_DIM = 128
ROPE_THETA = 10000.0
DILATED_GROUPS = ((128, 1), (512, 4), (2048, 16))
HEADS_PER_GROUP = 4
A_WIDTH = 1536
A_OUT_WIDTH = HEADS_PER_GROUP * HEAD_DIM
ATTN_BLOCK = 128

DN_HEADS = 16
DN_DK = 128
DN_CONV = 4
DN_CHUNK = 64
DN_K_WIDTH = DN_HEADS * DN_DK
DN_QKV_WIDTH = 3 * DN_K_WIDTH

PEER_HEADS = 8
PEER_KEYS = 128
PEER_EXPERTS = PEER_KEYS * PEER_KEYS
PEER_HALF = 128
PEER_TOPK = 16

P_GATE_A = 0
P_GATE_D = 4096
P_DN_QKV = 8192
P_DN_Z = 14336
P_ATTN_Q = 16384
P_ATTN_K = 17920
P_ATTN_V = 19456
P_WIDTH = 20992
W_ATTN = 0
W_DN = 3 * A_WIDTH
W_Z = W_DN + DN_QKV_WIDTH
W_SMALL = W_Z + DN_K_WIDTH
W_GATES = W_SMALL + 2 * DN_HEADS

LANES = 128
VMEM_LIMIT = 52 * 1024 * 1024

_ARB1 = ("arbitrary",)
_ARB2 = ("arbitrary", "arbitrary")


def _params(sem):
    return pltpu.CompilerParams(dimension_semantics=sem, vmem_limit_bytes=VMEM_LIMIT)


def _sigmoid(x):
    return 1.0 / (1.0 + jnp.exp(-x))


def _dot(a, b):
    return jnp.dot(a, b, preferred_element_type=jnp.float32)


def _dot_nt(a, b):
    return lax.dot_general(a, b, (((1,), (1,)), ((), ())), preferred_element_type=jnp.float32)


def _dot_tn(a, b):
    return lax.dot_general(a, b, (((0,), (0,)), ((), ())), preferred_element_type=jnp.float32)


def _rms(x, w):
    return x * lax.rsqrt(jnp.mean(x * x, axis=-1, keepdims=True) + NORM_EPS) * w


def _adaln_kernel(c_ref, w_ref, b_ref, o_ref):
    c = c_ref[...]
    act = c * _sigmoid(c)
    o_ref[...] = jnp.sum(act * w_ref[...], axis=0, keepdims=True) + b_ref[...]


def _adaln(c_col, w, b, tn=512):
    d, n = w.shape
    return pl.pallas_call(
        _adaln_kernel,
        grid=(n // tn,),
        in_specs=[pl.BlockSpec((d, 1), lambda j: (0, 0)),
                  pl.BlockSpec((d, tn), lambda j: (0, j)),
                  pl.BlockSpec((1, tn), lambda j: (0, j))],
        out_specs=pl.BlockSpec((1, tn), lambda j: (0, j)),
        out_shape=jax.ShapeDtypeStruct((1, n), jnp.float32),
        compiler_params=_params(_ARB1),
        name="adaln",
    )(c_col, w, b)


def _prenorm_kernel(x_ref, nw_ref, shift_ref, scale_ref, o_ref):
    y = _rms(x_ref[...], nw_ref[...])
    o_ref[...] = (y * (1.0 + scale_ref[...]) + shift_ref[...]).astype(o_ref.dtype)


def _prenorm(x, nw, shift, scale, tm=256):
    s, d = x.shape
    row = pl.BlockSpec((1, d), lambda i: (0, 0))
    return pl.pallas_call(
        _prenorm_kernel,
        grid=(s // tm,),
        in_specs=[pl.BlockSpec((tm, d), lambda i: (i, 0)), row, row, row],
        out_specs=pl.BlockSpec((tm, d), lambda i: (i, 0)),
        out_shape=jax.ShapeDtypeStruct((s, d), jnp.bfloat16),
        compiler_params=_params(_ARB1),
        name="prenorm",
    )(x, nw, shift, scale)


def _mm_kernel(a_ref, b_ref, o_ref):
    o_ref[...] = _dot(a_ref[...], b_ref[...]).astype(o_ref.dtype)


def _matmul(a, b, out_dtype, tm, tn, name):
    m, k = a.shape
    n = b.shape[1]
    return pl.pallas_call(
        _mm_kernel,
        grid=(m // tm, n // tn),
        in_specs=[pl.BlockSpec((tm, k), lambda i, j: (i, 0)),
                  pl.BlockSpec((k, tn), lambda i, j: (0, j))],
        out_specs=pl.BlockSpec((tm, tn), lambda i, j: (i, j)),
        out_shape=jax.ShapeDtypeStruct((m, n), out_dtype),
        compiler_params=_params(_ARB2),
        name=name,
    )(a, b)


def _small_kernel(h_ref, wt_ref, ot_ref):
    ot_ref[...] = _dot_nt(wt_ref[...], h_ref[...])


def _small_proj(h, wt, tm=512):
    s, d = h.shape
    r = wt.shape[0]
    return pl.pallas_call(
        _small_kernel,
        grid=(s // tm,),
        in_specs=[pl.BlockSpec((tm, d), lambda i: (i, 0)),
                  pl.BlockSpec((r, d), lambda i: (0, 0))],
        out_specs=pl.BlockSpec((r, tm), lambda i: (0, i)),
        out_shape=jax.ShapeDtypeStruct((r, s), jnp.float32),
        compiler_params=_params(_ARB1),
        name="small_proj",
    )(h, wt)


def _rope(x, cos, sin):
    return x * cos + pltpu.roll(x, HEAD_DIM // 2, 1) * sin


def _attn_kernel(qc_ref, kc_ref, vc_ref, kp_ref, vp_ref, cc_ref, sc_ref, cp_ref, sp_ref, o_ref, l_ref):
    n = pl.program_id(1)
    cos_c, sin_c, cos_p, sin_p = cc_ref[...], sc_ref[...], cp_ref[...], sp_ref[...]
    qi = lax.broadcasted_iota(jnp.int32, (ATTN_BLOCK, ATTN_BLOCK), 0)
    kj = lax.broadcasted_iota(jnp.int32, (ATTN_BLOCK, ATTN_BLOCK), 1)
    mask_c = kj <= qi
    mask_p = jnp.logical_and(kj >= qi, n > 0)
    scale = HEAD_DIM ** -0.5
    for hh in range(HEADS_PER_GROUP):
        sl = slice(hh * HEAD_DIM, (hh + 1) * HEAD_DIM)
        q = _rope(qc_ref[:, sl], cos_c, sin_c).astype(jnp.bfloat16)
        k_c = _rope(kc_ref[:, sl], cos_c, sin_c).astype(jnp.bfloat16)
        k_p = _rope(kp_ref[:, sl], cos_p, sin_p).astype(jnp.bfloat16)
        s_c = jnp.where(mask_c, _dot_nt(q, k_c) * scale, NEG_INF)
        s_p = jnp.where(mask_p, _dot_nt(q, k_p) * scale, NEG_INF)
        m = jnp.maximum(jnp.max(s_c, axis=-1, keepdims=True), jnp.max(s_p, axis=-1, keepdims=True))
        p_c = jnp.exp(s_c - m)
        p_p = jnp.exp(s_p - m)
        den = jnp.sum(p_c, axis=-1, keepdims=True) + jnp.sum(p_p, axis=-1, keepdims=True)
        pv = (_dot(p_c.astype(jnp.bfloat16), vc_ref[:, sl].astype(jnp.bfloat16))
              + _dot(p_p.astype(jnp.bfloat16), vp_ref[:, sl].astype(jnp.bfloat16)))
        o_ref[:, sl] = pv / den
        l_ref[:, sl] = jnp.broadcast_to(m + jnp.log(den), (ATTN_BLOCK, HEAD_DIM))


def _attention_group(p, cos, sin, g, dilation):
    s = p.shape[0]
    l = s // dilation
    nb = l // ATTN_BLOCK
    pv = p.reshape(l, dilation * P_WIDTH)
    cv = cos.reshape(l, dilation * HEAD_DIM)
    sv = sin.reshape(l, dilation * HEAD_DIM)
    cpb = P_WIDTH // A_OUT_WIDTH
    qb, kb, vb = (P_ATTN_Q // A_OUT_WIDTH + g, P_ATTN_K // A_OUT_WIDTH + g, P_ATTN_V // A_OUT_WIDTH + g)

    def cur(off):
        return pl.BlockSpec((ATTN_BLOCK, A_OUT_WIDTH), lambda r, n: (n, r * cpb + off))

    def prev(off):
        return pl.BlockSpec((ATTN_BLOCK, A_OUT_WIDTH), lambda r, n: (jnp.maximum(n - 1, 0), r * cpb + off))

    tab_c = pl.BlockSpec((ATTN_BLOCK, HEAD_DIM), lambda r, n: (n, r))
    tab_p = pl.BlockSpec((ATTN_BLOCK, HEAD_DIM), lambda r, n: (jnp.maximum(n - 1, 0), r))
    out = pl.BlockSpec((ATTN_BLOCK, A_OUT_WIDTH), lambda r, n: (n, r))
    o, lse = pl.pallas_call(
        _attn_kernel,
        grid=(dilation, nb),
        in_specs=[cur(qb), cur(kb), cur(vb), prev(kb), prev(vb), tab_c, tab_c, tab_p, tab_p],
        out_specs=[out, out],
        out_shape=[jax.ShapeDtypeStruct((l, dilation * A_OUT_WIDTH), jnp.float32)] * 2,
        compiler_params=_params(_ARB2),
        name=f"attn_d{dilation}",
    )(pv, pv, pv, pv, pv, cv, sv, cv, sv)
    return o.reshape(s, A_OUT_WIDTH), lse.reshape(s, A_OUT_WIDTH)


DN_PREP_COLS = 512
_QK_BLOCKS = 2 * DN_K_WIDTH // DN_PREP_COLS


def _dnprep_kernel(x_ref, halo_ref, w_ref, o_ref):
    i = pl.program_id(0)
    j = pl.program_id(1)
    x = x_ref[...]
    tm = x.shape[0]
    halo = jnp.where(i > 0, halo_ref[...], 0.0)
    xx = jnp.concatenate([halo, x], axis=0)
    w = w_ref[...]
    acc = x * w[DN_CONV - 1:DN_CONV, :]
    for sft in range(1, DN_CONV):
        acc = acc + xx[8 - sft:8 - sft + tm, :] * w[DN_CONV - 1 - sft:DN_CONV - sft, :]
    y = acc * _sigmoid(acc)
    for hh in range(DN_PREP_COLS // DN_DK):
        sl = slice(hh * DN_DK, (hh + 1) * DN_DK)
        yh = y[:, sl]
        f = lax.rsqrt(jnp.sum(yh * yh, axis=-1, keepdims=True) + NORM_EPS)
        f = jnp.where(j < _QK_BLOCKS, f, 1.0)
        o_ref[:, sl] = (yh * f).astype(o_ref.dtype)


def _dn_prep(p, conv_w, tm=512):
    s = p.shape[0]
    cb0 = P_DN_QKV // DN_PREP_COLS
    return pl.pallas_call(
        _dnprep_kernel,
        grid=(s // tm, DN_QKV_WIDTH // DN_PREP_COLS),
        in_specs=[pl.BlockSpec((tm, DN_PREP_COLS), lambda i, j: (i, cb0 + j)),
                  pl.BlockSpec((8, DN_PREP_COLS), lambda i, j: (jnp.maximum(i * (tm // 8) - 1, 0), cb0 + j)),
                  pl.BlockSpec((DN_CONV, DN_PREP_COLS), lambda i, j: (0, j))],
        out_specs=pl.BlockSpec((tm, DN_PREP_COLS), lambda i, j: (i, j)),
        out_shape=jax.ShapeDtypeStruct((s, DN_QKV_WIDTH), jnp.bfloat16),
        compiler_params=_params(_ARB2),
        name="dn_prep",
    )(p, p, conv_w)


DELTA_ROWS = 2 * DN_CHUNK
DELTA_HG = 4


def _softplus(x):
    return jnp.maximum(x, 0.0) + jnp.log1p(jnp.exp(-jnp.abs(x)))


def _delta_kernel(q_ref, k_ref, v_ref, z_ref, braw_ref, araw_ref, alog_ref, dtb_ref, nw_ref, o_ref, state_ref):
    step = pl.program_id(0)
    hg = pl.program_id(1)
    c = DN_CHUNK
    hi = lax.Precision.HIGHEST

    @pl.when(step == 0)
    def _():
        for hh in range(DELTA_HG):
            state_ref[hg * DELTA_HG + hh] = jnp.zeros(state_ref.shape[1:], jnp.float32)

    beta_t = _sigmoid(braw_ref[...])
    g_t = -jnp.exp(alog_ref[...]) * _softplus(araw_ref[...] + dtb_ref[...])

    ii = lax.broadcasted_iota(jnp.int32, (c, c), 0)
    jj = lax.broadcasted_iota(jnp.int32, (c, c), 1)
    lower = (ii >= jj).astype(jnp.float32)
    upper = (ii <= jj).astype(jnp.float32)
    eye = (ii == jj).astype(jnp.float32)
    nw = nw_ref[...]
    scale = DN_DK ** -0.5

    for cc in range(DELTA_ROWS // c):
        rows = slice(cc * c, (cc + 1) * c)
        g_c = g_t[:, rows]
        b_c = beta_t[:, rows]
        gcum_row = jnp.dot(g_c, upper, precision=hi, preferred_element_type=jnp.float32)
        gcum_col = lax.dot_general(lower, g_c, (((1,), (1,)), ((), ())), precision=hi,
                                   preferred_element_type=jnp.float32)
        beta_col = lax.dot_general(eye, b_c, (((1,), (1,)), ((), ())), precision=hi,
                                   preferred_element_type=jnp.float32)
        for hh in range(DELTA_HG):
            cols = slice(hh * DN_DK, (hh + 1) * DN_DK)
            q = q_ref[rows, cols].astype(jnp.float32)
            k = k_ref[rows, cols].astype(jnp.float32)
            v = v_ref[rows, cols].astype(jnp.float32)
            gcol = gcum_col[:, hh:hh + 1]
            grow = gcum_row[hh:hh + 1, :]
            bcol = beta_col[:, hh:hh + 1]
            state = state_ref[hg * DELTA_HG + hh]

            dec = jnp.exp(jnp.minimum(gcol - grow, 0.0))
            k16 = k.astype(jnp.bfloat16)
            both = _dot_nt(jnp.concatenate([q, k], axis=0).astype(jnp.bfloat16), k16)
            a_in = jnp.where(ii >= jj, both[:c] * dec, 0.0) * scale
            lm = jnp.where(ii > jj, both[c:] * dec * bcol, 0.0)
            pw = -lm
            t = eye + pw
            for _ in range(5):
                p16 = pw.astype(jnp.bfloat16)
                pw = _dot(p16, p16)
                t = t + _dot(t.astype(jnp.bfloat16), pw.astype(jnp.bfloat16))
            eg = jnp.exp(gcol)
            rhs = jnp.concatenate([v * bcol, k * (bcol * eg)], axis=1).astype(jnp.bfloat16)
            uw = _dot(t.astype(jnp.bfloat16), rhs)
            lhs = jnp.concatenate([uw[:, DN_DK:], q * (scale * eg)], axis=0).astype(jnp.bfloat16)
            wq = _dot(lhs, state.astype(jnp.bfloat16))
            v_new = uw[:, :DN_DK] - wq[:c]
            v16 = v_new.astype(jnp.bfloat16)
            o = wq[c:] + _dot(a_in.astype(jnp.bfloat16), v16)
            g_last = gcol[c - 1:c, :]
            k_dec = (k * jnp.exp(g_last - gcol)).astype(jnp.bfloat16)
            state_ref[hg * DELTA_HG + hh] = state * jnp.exp(g_last) + _dot_tn(k_dec, v16)

            z = z_ref[rows, cols]
            o_ref[rows, cols] = (_rms(o, nw) * (z * _sigmoid(z))).astype(o_ref.dtype)


def _delta(qkvn, p, small_t, a_log, dt_bias, norm_w):
    s = qkvn.shape[0]
    w = DELTA_HG * DN_DK
    nhg = DN_HEADS // DELTA_HG
    zb0 = P_DN_Z // w
    small_g = small_t.reshape(2 * nhg, DELTA_HG, s)
    a_log_g = a_log.reshape(nhg, DELTA_HG, 1)
    dt_bias_g = dt_bias.reshape(nhg, DELTA_HG, 1)

    def blk(off):
        return pl.BlockSpec((DELTA_ROWS, w), lambda i, j: (i, off + j))

    def head_rows(off):
        return pl.BlockSpec((None, DELTA_HG, DELTA_ROWS), lambda i, j: (off + j, 0, i))

    per_head = pl.BlockSpec((None, DELTA_HG, 1), lambda i, j: (j, 0, 0))
    return pl.pallas_call(
        _delta_kernel,
        grid=(s // DELTA_ROWS, nhg),
        in_specs=[blk(0), blk(nhg), blk(2 * nhg), blk(zb0), head_rows(0), head_rows(nhg), per_head, per_head,
                  pl.BlockSpec((1, DN_DK), lambda i, j: (0, 0))],
        out_specs=pl.BlockSpec((DELTA_ROWS, w), lambda i, j: (i, j)),
        out_shape=jax.ShapeDtypeStruct((s, DN_K_WIDTH), jnp.bfloat16),
        scratch_shapes=[pltpu.VMEM((DN_HEADS, DN_DK, DN_DK), jnp.float32)],
        compiler_params=_params(_ARB2),
        name="delta",
    )(qkvn, qkvn, qkvn, p, small_g, small_g, a_log_g, dt_bias_g, norm_w)


def _merge_kernel(o0, o1, o2, l0, l1, l2, yd_ref, woa_ref, wod_ref, ga_ref, gd_ref, out_ref, ya_ref):
    @pl.when(pl.program_id(1) == 0)
    def _():
        la, lb, lc = l0[...], l1[...], l2[...]
        m = jnp.maximum(jnp.maximum(la, lb), lc)
        ea, eb, ec = jnp.exp(la - m), jnp.exp(lb - m), jnp.exp(lc - m)
        ya = (o0[...] * ea + o1[...] * eb + o2[...] * ec) / (ea + eb + ec)
        ya_ref[...] = ya.astype(ya_ref.dtype)

    acc_a = _dot(ya_ref[...], woa_ref[...])
    acc_d = _dot(yd_ref[...], wod_ref[...])
    out_ref[...] = (_sigmoid(ga_ref[...]) * acc_a + _sigmoid(gd_ref[...]) * acc_d).astype(out_ref.dtype)


def _merge(os_, ls_, y_d, w_oa, w_od, p, tm=512, tn=1024):
    s = y_d.shape[0]
    n = w_oa.shape[1]
    att = pl.BlockSpec((tm, A_OUT_WIDTH), lambda i, j: (i, 0))
    gdb = P_GATE_D // tn
    return pl.pallas_call(
        _merge_kernel,
        grid=(s // tm, n // tn),
        in_specs=[att] * 6 + [
            pl.BlockSpec((tm, DN_K_WIDTH), lambda i, j: (i, 0)),
            pl.BlockSpec((A_OUT_WIDTH, tn), lambda i, j: (0, j)),
            pl.BlockSpec((DN_K_WIDTH, tn), lambda i, j: (0, j)),
            pl.BlockSpec((tm, tn), lambda i, j: (i, j)),
            pl.BlockSpec((tm, tn), lambda i, j: (i, gdb + j))],
        out_specs=pl.BlockSpec((tm, tn), lambda i, j: (i, j)),
        out_shape=jax.ShapeDtypeStruct((s, n), jnp.bfloat16),
        scratch_shapes=[pltpu.VMEM((tm, A_OUT_WIDTH), jnp.bfloat16)],
        compiler_params=_params(_ARB2),
        name="merge",
    )(*os_, *ls_, y_d, w_oa, w_od, p, p)


def _postmix_kernel(mix_ref, x_ref, gate_ref, pn_ref, fn_ref, shift_ref, scale_ref, x1_ref, h_ref):
    x1 = x_ref[...] + gate_ref[...] * _rms(mix_ref[...], pn_ref[...])
    x1_ref[...] = x1
    h_ref[...] = (_rms(x1, fn_ref[...]) * (1.0 + scale_ref[...]) + shift_ref[...]).astype(h_ref.dtype)


def _post_mix(mix, x, gate, post_w, ffn_w, shift, scale, tm=256):
    s, d = x.shape
    row = pl.BlockSpec((1, d), lambda i: (0, 0))
    blk = pl.BlockSpec((tm, d), lambda i: (i, 0))
    return pl.pallas_call(
        _postmix_kernel,
        grid=(s // tm,),
        in_specs=[blk, blk, row, row, row, row, row],
        out_specs=[blk, blk],
        out_shape=[jax.ShapeDtypeStruct((s, d), jnp.float32), jax.ShapeDtypeStruct((s, d), jnp.bfloat16)],
        compiler_params=_params(_ARB1),
        name="post_mix",
    )(mix, x, gate, post_w, ffn_w, shift, scale)


def _final_kernel(y_ref, x_ref, gate_ref, pn_ref, o_ref):
    o_ref[...] = x_ref[...] + gate_ref[...] * _rms(y_ref[...], pn_ref[...])


def _final(y, x1, gate, post_w, tm=256):
    s, d = x1.shape
    row = pl.BlockSpec((1, d), lambda i: (0, 0))
    blk = pl.BlockSpec((tm, d), lambda i: (i, 0))
    return pl.pallas_call(
        _final_kernel,
        grid=(s // tm,),
        in_specs=[blk, blk, row, row],
        out_specs=blk,
        out_shape=jax.ShapeDtypeStruct((s, d), jnp.float32),
        compiler_params=_params(_ARB1),
        name="final",
    )(y, x1, gate, post_w)


ROUTER_TM = 256


def _topk_desc(scores):
    work = scores
    vals = []
    for it in range(PEER_TOPK):
        mx = jnp.max(work, axis=0, keepdims=True)
        vals.append(mx)
        if it + 1 < PEER_TOPK:
            work = jnp.where(work == mx, -jnp.inf, work)
    return jnp.concatenate(vals, axis=0)


def _router_kernel(q_ref, k1_ref, k2_ref, s1_ref, e1_ref, s2_ref, e2_ref, thr_ref):
    k1 = k1_ref[...].astype(jnp.bfloat16)
    k2 = k2_ref[...].astype(jnp.bfloat16)

    def head(h, carry):
        c0 = pl.multiple_of(h * 2 * PEER_HALF, 2 * PEER_HALF)
        q1 = q_ref[:, pl.ds(c0, PEER_HALF)].astype(jnp.bfloat16)
        q2 = q_ref[:, pl.ds(c0 + PEER_HALF, PEER_HALF)].astype(jnp.bfloat16)
        s1 = _dot_nt(k1, q1)
        s2 = _dot_nt(k2, q2)
        v1 = _topk_desc(s1)
        v2 = _topk_desc(s2)
        cand = jnp.concatenate([v1[i:i + 1, :] + v2 for i in range(PEER_TOPK)], axis=0)
        thr = _topk_desc(cand)[PEER_TOPK - 1:PEER_TOPK, :]
        top = v1[0:1, :] + v2[0:1, :]
        zsum = jnp.sum(jnp.where(cand >= thr, jnp.exp(cand - top), 0.0), axis=0, keepdims=True)
        s1_ref[h] = s1
        e1_ref[h] = jnp.exp(s1 - v1[0:1, :]) / zsum
        s2_ref[h] = s2
        e2_ref[h] = jnp.exp(s2 - v2[0:1, :])
        thr_ref[pl.ds(h, 1), :] = thr
        return carry

    lax.fori_loop(0, PEER_HEADS, head, 0)


def _router(q, keys_1, keys_2):
    s = q.shape[0]
    tm = ROUTER_TM
    h_major = pl.BlockSpec((PEER_HEADS, PEER_KEYS, tm), lambda i: (0, 0, i))
    kspec = pl.BlockSpec((PEER_KEYS, PEER_HALF), lambda i: (0, 0))
    return pl.pallas_call(
        _router_kernel,
        grid=(s // tm,),
        in_specs=[pl.BlockSpec((tm, 2 * PEER_HALF * PEER_HEADS), lambda i: (i, 0)), kspec, kspec],
        out_specs=[h_major] * 4 + [pl.BlockSpec((PEER_HEADS, tm), lambda i: (0, i))],
        out_shape=[jax.ShapeDtypeStruct((PEER_HEADS, PEER_KEYS, s), jnp.float32)] * 4
        + [jax.ShapeDtypeStruct((PEER_HEADS, s), jnp.float32)],
        compiler_params=_params(_ARB1),
        name="router",
    )(q, keys_1, keys_2)


PEER_TM = 512
PEER_EC = 512
_A_PER_STEP = PEER_EC // PEER_KEYS


def _gelu(x):
    return 0.5 * x * (1.0 + lax.erf(x * (2.0 ** -0.5)))


def _peer_kernel(h_ref, u_ref, v_ref, s1_ref, e1_ref, s2_ref, e2_ref, thr_ref, o_ref, act_ref):
    @pl.when(pl.program_id(1) == 0)
    def _():
        o_ref[...] = jnp.zeros(o_ref.shape, jnp.float32)

    sc = _dot_nt(u_ref[...], h_ref[...])
    tm = h_ref.shape[0]
    for al in range(_A_PER_STEP):
        for lt in range(tm // LANES):
            tok = slice(lt * LANES, (lt + 1) * LANES)
            gate = jnp.zeros((PEER_KEYS, LANES), jnp.float32)
            for h in range(PEER_HEADS):
                cand = s1_ref[al, h:h + 1, tok] + s2_ref[h, :, tok]
                w = e1_ref[al, h:h + 1, tok] * e2_ref[h, :, tok]
                gate = gate + jnp.where(cand >= thr_ref[h:h + 1, tok], w, 0.0)
            ex = slice(al * PEER_KEYS, (al + 1) * PEER_KEYS)
            act_ref[ex, tok] = (gate * _gelu(sc[ex, tok])).astype(act_ref.dtype)
    o_ref[...] += _dot_tn(act_ref[...], v_ref[...])


def _peer(h2, u16, v16, s1, e1, s2, e2, thr):
    s, d = h2.shape
    tm, ec = PEER_TM, PEER_EC
    a_major = pl.BlockSpec((_A_PER_STEP, PEER_HEADS, tm), lambda i, j: (j, 0, i))
    h_major = pl.BlockSpec((PEER_HEADS, PEER_KEYS, tm), lambda i, j: (0, 0, i))
    return pl.pallas_call(
        _peer_kernel,
        grid=(s // tm, PEER_EXPERTS // ec),
        in_specs=[pl.BlockSpec((tm, d), lambda i, j: (i, 0)),
                  pl.BlockSpec((ec, d), lambda i, j: (j, 0)),
                  pl.BlockSpec((ec, d), lambda i, j: (j, 0)),
                  a_major, a_major, h_major, h_major,
                  pl.BlockSpec((PEER_HEADS, tm), lambda i, j: (0, i))],
        out_specs=pl.BlockSpec((tm, d), lambda i, j: (i, 0)),
        out_shape=jax.ShapeDtypeStruct((s, d), jnp.float32),
        scratch_shapes=[pltpu.VMEM((ec, tm), jnp.bfloat16)],
        compiler_params=_params(_ARB2),
        name="peer",
    )(h2, u16, v16, s1, e1, s2, e2, thr)


def _rope_tables(positions):
    half = HEAD_DIM // 2
    inv_freq = ROPE_THETA ** (-jnp.arange(half, dtype=jnp.float32) / half)
    ang = positions.astype(jnp.float32)[:, None] * inv_freq[None, :]
    cos, sin = jnp.cos(ang), jnp.sin(ang)
    return jnp.concatenate([cos, cos], axis=-1), jnp.concatenate([-sin, sin], axis=-1)


def _layer(x, c_col, positions, w_ada, b_ada, attn_pre_norm, attn_post_norm, w_in, conv_w, a_log, dt_bias,
           dn_norm_w, w_o_attn, w_o_dn, w_out, ffn_pre_norm, ffn_post_norm, w_peer_q, peer_keys_1,
           peer_keys_2, expert_u, expert_v):
    d = D_MODEL
    bf = jnp.bfloat16
    mod = _adaln(c_col, w_ada, b_ada[None, :])
    shift_mix, scale_mix, gate_mix, shift_ffn, scale_ffn, gate_ffn = [mod[:, i * d:(i + 1) * d] for i in range(6)]

    h = _prenorm(x, attn_pre_norm[None, :], shift_mix, scale_mix)
    w_main = jnp.concatenate([w_in[:, W_GATES:], w_in[:, W_DN:W_SMALL], w_in[:, W_ATTN:W_DN]], axis=1).astype(bf)
    p = _matmul(h, w_main, jnp.float32, 1024, 512, "in_proj")
    small_t = _small_proj(h, w_in[:, W_SMALL:W_GATES].T.astype(bf))

    cos, sin = _rope_tables(positions)
    outs, lses = [], []
    for g, (_, dilation) in enumerate(DILATED_GROUPS):
        o, lse = _attention_group(p, cos, sin, g, dilation)
        outs.append(o)
        lses.append(lse)

    qkvn = _dn_prep(p, conv_w)
    y_d = _delta(qkvn, p, small_t, a_log, dt_bias, dn_norm_w[None, :])

    merged = _merge(outs, lses, y_d, w_o_attn.astype(bf), w_o_dn.astype(bf), p)
    mix = _matmul(merged, w_out.astype(bf), jnp.float32, 1024, 512, "out_proj")
    x1, h2 = _post_mix(mix, x, gate_mix, attn_post_norm[None, :], ffn_pre_norm[None, :], shift_ffn, scale_ffn)

    q = _matmul(h2, w_peer_q.astype(bf), jnp.float32, 1024, 512, "peer_q")
    s1, e1, s2, e2, thr = _router(q, peer_keys_1, peer_keys_2)
    s1, e1 = jnp.transpose(s1, (1, 0, 2)), jnp.transpose(e1, (1, 0, 2))
    y = _peer(h2, expert_u.astype(bf), expert_v.astype(bf), s1, e1, s2, e2, thr)
    return _final(y, x1, gate_ffn, ffn_post_norm[None, :])


def kernel(x, c, positions, w_ada, b_ada, attn_pre_norm, attn_post_norm, w_in, conv_w, a_log, dt_bias, dn_norm_w,
           w_o_attn, w_o_dn, w_out, ffn_pre_norm, ffn_post_norm, w_peer_q, peer_keys_1, peer_keys_2, expert_u,
           expert_v):
    batch, seq, d = x.shape
    depth = w_ada.shape[0]
    outs = []
    for b in range(batch):
        xb = x[b]
        c_col = c[b][:, None]
        for l in range(depth):
            xb = _layer(xb, c_col, positions[b], w_ada[l], b_ada[l], attn_pre_norm[l], attn_post_norm[l], w_in[l],
                        conv_w[l], a_log[l], dt_bias[l], dn_norm_w[l], w_o_attn[l], w_o_dn[l], w_out[l],
                        ffn_pre_norm[l], ffn_post_norm[l], w_peer_q[l], peer_keys_1[l], peer_keys_2[l],
                        expert_u[l], expert_v[l])
        outs.append(xb)
    return jnp.stack(outs, axis=0)
```

```python
import functools

import jax
import jax.numpy as jnp
from jax import lax
from jax.experimental import pallas as pl
from jax.experimental.pallas import tpu as pltpu

D_MODEL = 4096
NORM_EPS = 1e-6
NEG_INF = -1e30

HEAD_DIM = 128
ROPE_THETA = 10000.0
DILATED_GROUPS = ((128, 1), (512, 4), (2048, 16))
HEADS_PER_GROUP = 4
A_WIDTH = 1536
A_OUT_WIDTH = HEADS_PER_GROUP * HEAD_DIM
ATTN_BLOCK = 128
ATTN_SPAN = 2048

DN_HEADS = 16
DN_DK = 128
DN_CONV = 4
DN_CHUNK = 64
DN_K_WIDTH = DN_HEADS * DN_DK
DN_QKV_WIDTH = 3 * DN_K_WIDTH

PEER_HEADS = 8
PEER_KEYS = 128
PEER_EXPERTS = PEER_KEYS * PEER_KEYS
PEER_HALF = 128
PEER_TOPK = 16

P_GATE_A = 0
P_GATE_D = 4096
P_DN_QKV = 8192
P_DN_Z = 14336
P_ATTN_Q = 16384
P_ATTN_K = 17920
P_ATTN_V = 19456
P_WIDTH = 20992
W_ATTN = 0
W_DN = 3 * A_WIDTH
W_Z = W_DN + DN_QKV_WIDTH
W_SMALL = W_Z + DN_K_WIDTH
W_GATES = W_SMALL + 2 * DN_HEADS

LANES = 128
VMEM_LIMIT = 52 * 1024 * 1024

_ARB1 = ("arbitrary",)
_ARB2 = ("arbitrary", "arbitrary")


def _params(sem):
    return pltpu.CompilerParams(dimension_semantics=sem, vmem_limit_bytes=VMEM_LIMIT)


def _sigmoid(x):
    return 1.0 / (1.0 + jnp.exp(-x))


def _dot(a, b):
    return jnp.dot(a, b, preferred_element_type=jnp.float32)


def _dot_nt(a, b):
    return lax.dot_general(a, b, (((1,), (1,)), ((), ())), preferred_element_type=jnp.float32)


def _dot_tn(a, b):
    return lax.dot_general(a, b, (((0,), (0,)), ((), ())), preferred_element_type=jnp.float32)


def _rms(x, w):
    return x * lax.rsqrt(jnp.mean(x * x, axis=-1, keepdims=True) + NORM_EPS) * w


def _adaln_kernel(c_ref, w_ref, b_ref, o_ref):
    c = c_ref[...]
    act = c * _sigmoid(c)
    o_ref[...] = jnp.sum(act * w_ref[...], axis=0, keepdims=True) + b_ref[...]


def _adaln(c_col, w, b, tn=512):
    d, n = w.shape
    return pl.pallas_call(
        _adaln_kernel,
        grid=(n // tn,),
        in_specs=[pl.BlockSpec((d, 1), lambda j: (0, 0)),
                  pl.BlockSpec((d, tn), lambda j: (0, j)),
                  pl.BlockSpec((1, tn), lambda j: (0, j))],
        out_specs=pl.BlockSpec((1, tn), lambda j: (0, j)),
        out_shape=jax.ShapeDtypeStruct((1, n), jnp.float32),
        compiler_params=_params(_ARB1),
        name="adaln",
    )(c_col, w, b)


def _prenorm_kernel(x_ref, nw_ref, shift_ref, scale_ref, o_ref):
    y = _rms(x_ref[...], nw_ref[...])
    o_ref[...] = (y * (1.0 + scale_ref[...]) + shift_ref[...]).astype(o_ref.dtype)


def _prenorm(x, nw, shift, scale, tm=256):
    s, d = x.shape
    row = pl.BlockSpec((1, d), lambda i: (0, 0))
    return pl.pallas_call(
        _prenorm_kernel,
        grid=(s // tm,),
        in_specs=[pl.BlockSpec((tm, d), lambda i: (i, 0)), row, row, row],
        out_specs=pl.BlockSpec((tm, d), lambda i: (i, 0)),
        out_shape=jax.ShapeDtypeStruct((s, d), jnp.bfloat16),
        compiler_params=_params(_ARB1),
        name="prenorm",
    )(x, nw, shift, scale)


def _mm_kernel(a_ref, b_ref, o_ref):
    o_ref[...] = _dot(a_ref[...], b_ref[...]).astype(o_ref.dtype)


def _matmul(a, b, out_dtype, tm, tn, name):
    m, k = a.shape
    n = b.shape[1]
    return pl.pallas_call(
        _mm_kernel,
        grid=(m // tm, n // tn),
        in_specs=[pl.BlockSpec((tm, k), lambda i, j: (i, 0)),
                  pl.BlockSpec((k, tn), lambda i, j: (0, j))],
        out_specs=pl.BlockSpec((tm, tn), lambda i, j: (i, j)),
        out_shape=jax.ShapeDtypeStruct((m, n), out_dtype),
        compiler_params=_params(_ARB2),
        name=name,
    )(a, b)


def _small_kernel(h_ref, wt_ref, ot_ref):
    ot_ref[...] = _dot_nt(wt_ref[...], h_ref[...])


def _small_proj(h, wt, tm=512):
    s, d = h.shape
    r = wt.shape[0]
    return pl.pallas_call(
        _small_kernel,
        grid=(s // tm,),
        in_specs=[pl.BlockSpec((tm, d), lambda i: (i, 0)),
                  pl.BlockSpec((r, d), lambda i: (0, 0))],
        out_specs=pl.BlockSpec((r, tm), lambda i: (0, i)),
        out_shape=jax.ShapeDtypeStruct((r, s), jnp.float32),
        compiler_params=_params(_ARB1),
        name="small_proj",
    )(h, wt)


def _rope(x, cos, sin):
    return x * cos + pltpu.roll(x, HEAD_DIM // 2, 1) * sin


def _attn_kernel(qc_ref, kc_ref, vc_ref, kp_ref, vp_ref, cc_ref, sc_ref, cp_ref, sp_ref, o_ref, l_ref, *, dilation):
    n = pl.program_id(0)
    sub = ATTN_BLOCK * dilation
    qi = lax.broadcasted_iota(jnp.int32, (ATTN_BLOCK, ATTN_BLOCK), 0)
    kj = lax.broadcasted_iota(jnp.int32, (ATTN_BLOCK, ATTN_BLOCK), 1)
    mask_c = kj <= qi
    mask_in = kj >= qi
    mask_first = jnp.logical_and(mask_in, n > 0)
    scale = HEAD_DIM ** -0.5

    def rows_of(sb, r):
        return pl.ds(sb * sub + r, ATTN_BLOCK, stride=dilation) if dilation > 1 else pl.ds(sb * sub, ATTN_BLOCK)

    def scores(sb, r):
        rows = rows_of(sb, r)
        cos_c, sin_c = cc_ref[rows, :], sc_ref[rows, :]
        if sb == 0:
            prow = rows_of(0, r)
            k_prev, v_prev, cos_p, sin_p, mask_p = kp_ref[prow, :], vp_ref[prow, :], cp_ref[prow, :], sp_ref[prow, :], mask_first
        else:
            prow = rows_of(sb - 1, r)
            k_prev, v_prev, cos_p, sin_p, mask_p = kc_ref[prow, :], vc_ref[prow, :], cc_ref[prow, :], sc_ref[prow, :], mask_in
        q = _rope(qc_ref[rows, :], cos_c, sin_c).astype(jnp.bfloat16)
        k_c = _rope(kc_ref[rows, :], cos_c, sin_c).astype(jnp.bfloat16)
        k_p = _rope(k_prev, cos_p, sin_p).astype(jnp.bfloat16)
        s_c = jnp.where(mask_c, _dot_nt(q, k_c) * scale, NEG_INF)
        s_p = jnp.where(mask_p, _dot_nt(q, k_p) * scale, NEG_INF)
        return rows, s_c, s_p, v_prev

    def finish(rows, s_c, s_p, v_prev):
        m = jnp.maximum(jnp.max(s_c, axis=-1, keepdims=True), jnp.max(s_p, axis=-1, keepdims=True))
        p_c = jnp.exp(s_c - m)
        p_p = jnp.exp(s_p - m)
        den = jnp.sum(p_c, axis=-1, keepdims=True) + jnp.sum(p_p, axis=-1, keepdims=True)
        pv = (_dot(p_c.astype(jnp.bfloat16), vc_ref[rows, :].astype(jnp.bfloat16))
              + _dot(p_p.astype(jnp.bfloat16), v_prev.astype(jnp.bfloat16)))
        o_ref[rows, :] = pv / den
        l_ref[rows, :] = jnp.broadcast_to(m + jnp.log(den), (ATTN_BLOCK, HEAD_DIM))

    items = [(sb, r) for sb in range(ATTN_SPAN // sub) for r in range(dilation)]
    batch = 4
    for b0 in range(0, len(items), batch):
        staged = [scores(sb, r) for sb, r in items[b0:b0 + batch]]
        for st in staged:
            finish(*st)


def _attention_group(p, cos, sin, g, dilation):
    s = p.shape[0]
    span = ATTN_SPAN
    sub = ATTN_BLOCK * dilation
    per = span // sub
    qb, kb, vb = [(off + g * A_OUT_WIDTH) // HEAD_DIM for off in (P_ATTN_Q, P_ATTN_K, P_ATTN_V)]

    def cur(off):
        return pl.BlockSpec((span, HEAD_DIM), lambda n, h: (n, off + h))

    def prev(off):
        return pl.BlockSpec((sub, HEAD_DIM), lambda n, h: (jnp.maximum(n * per - 1, 0), off + h))

    tab_c = pl.BlockSpec((span, HEAD_DIM), lambda n, h: (n, 0))
    tab_p = pl.BlockSpec((sub, HEAD_DIM), lambda n, h: (jnp.maximum(n * per - 1, 0), 0))
    out = pl.BlockSpec((span, HEAD_DIM), lambda n, h: (n, h))
    return pl.pallas_call(
        functools.partial(_attn_kernel, dilation=dilation),
        grid=(s // span, HEADS_PER_GROUP),
        in_specs=[cur(qb), cur(kb), cur(vb), prev(kb), prev(vb), tab_c, tab_c, tab_p, tab_p],
        out_specs=[out, out],
        out_shape=[jax.ShapeDtypeStruct((s, A_OUT_WIDTH), jnp.float32)] * 2,
        compiler_params=_params(_ARB2),
        name=f"attn_d{dilation}",
    )(p, p, p, p, p, cos, sin, cos, sin)


DN_PREP_COLS = 512
_QK_BLOCKS = 2 * DN_K_WIDTH // DN_PREP_COLS


def _dnprep_kernel(x_ref, halo_ref, w_ref, o_ref):
    i = pl.program_id(0)
    j = pl.program_id(1)
    x = x_ref[...]
    tm = x.shape[0]
    halo = jnp.where(i > 0, halo_ref[...], 0.0)
    xx = jnp.concatenate([halo, x], axis=0)
    w = w_ref[...]
    acc = x * w[DN_CONV - 1:DN_CONV, :]
    for sft in range(1, DN_CONV):
        acc = acc + xx[8 - sft:8 - sft + tm, :] * w[DN_CONV - 1 - sft:DN_CONV - sft, :]
    y = acc * _sigmoid(acc)
    for hh in range(DN_PREP_COLS // DN_DK):
        sl = slice(hh * DN_DK, (hh + 1) * DN_DK)
        yh = y[:, sl]
        f = lax.rsqrt(jnp.sum(yh * yh, axis=-1, keepdims=True) + NORM_EPS)
        f = jnp.where(j < _QK_BLOCKS, f, 1.0)
        o_ref[:, sl] = (yh * f).astype(o_ref.dtype)


def _dn_prep(p, conv_w, tm=512):
    s = p.shape[0]
    cb0 = P_DN_QKV // DN_PREP_COLS
    return pl.pallas_call(
        _dnprep_kernel,
        grid=(s // tm, DN_QKV_WIDTH // DN_PREP_COLS),
        in_specs=[pl.BlockSpec((tm, DN_PREP_COLS), lambda i, j: (i, cb0 + j)),
                  pl.BlockSpec((8, DN_PREP_COLS), lambda i, j: (jnp.maximum(i * (tm // 8) - 1, 0), cb0 + j)),
                  pl.BlockSpec((DN_CONV, DN_PREP_COLS), lambda i, j: (0, j))],
        out_specs=pl.BlockSpec((tm, DN_PREP_COLS), lambda i, j: (i, j)),
        out_shape=jax.ShapeDtypeStruct((s, DN_QKV_WIDTH), jnp.bfloat16),
        compiler_params=_params(_ARB2),
        name="dn_prep",
    )(p, p, conv_w)


DELTA_CHUNK = 128
DELTA_HG = 8


def _softplus(x):
    return jnp.maximum(x, 0.0) + jnp.log1p(jnp.exp(-jnp.abs(x)))


def _delta_local_kernel(q_ref, k_ref, v_ref, braw_ref, araw_ref, alog_ref, dtb_ref, u_ref, wq_ref, akt_ref, dl_ref):
    c = DELTA_CHUNK
    hi = lax.Precision.HIGHEST
    beta_t = _sigmoid(braw_ref[...])
    g_t = -jnp.exp(alog_ref[...]) * _softplus(araw_ref[...] + dtb_ref[...])

    ii = lax.broadcasted_iota(jnp.int32, (c, c), 0)
    jj = lax.broadcasted_iota(jnp.int32, (c, c), 1)
    lower = (ii >= jj).astype(jnp.float32)
    upper = (ii <= jj).astype(jnp.float32)
    eye = (ii == jj).astype(jnp.float32)
    scale = DN_DK ** -0.5

    gcum_row = jnp.dot(g_t, upper, precision=hi, preferred_element_type=jnp.float32)
    gcum_col = lax.dot_general(lower, g_t, (((1,), (1,)), ((), ())), precision=hi,
                               preferred_element_type=jnp.float32)
    beta_col = lax.dot_general(eye, beta_t, (((1,), (1,)), ((), ())), precision=hi,
                               preferred_element_type=jnp.float32)
    heads = range(DELTA_HG)
    cols = [slice(hh * DN_DK, (hh + 1) * DN_DK) for hh in heads]
    lms = []
    for hh in heads:
        q = q_ref[:, cols[hh]].astype(jnp.float32)
        k = k_ref[:, cols[hh]].astype(jnp.float32)
        gcol = gcum_col[:, hh:hh + 1]
        dec = jnp.exp(jnp.minimum(gcol - gcum_row[hh:hh + 1, :], 0.0))
        both = _dot_nt(jnp.concatenate([q, k], axis=0).astype(jnp.bfloat16), k.astype(jnp.bfloat16))
        g_last = gcol[c - 1:c, :]
        akt_ref[hh, :c, :] = (jnp.where(ii >= jj, both[:c] * dec, 0.0) * scale).astype(akt_ref.dtype)
        akt_ref[hh, c:, :] = (k * jnp.exp(g_last - gcol)).T.astype(akt_ref.dtype)
        wq_ref[hh, c:, :] = (q * (scale * jnp.exp(gcol))).astype(wq_ref.dtype)
        dl_ref[hh] = jnp.broadcast_to(jnp.exp(g_last), (1, DN_DK))
        lms.append(jnp.where(ii > jj, both[c:] * dec * beta_col[:, hh:hh + 1], 0.0))
    diff_bits = jnp.bitwise_xor(ii, jj)
    ts = [eye - jnp.where(diff_bits == 1, lm, 0.0) for lm in lms]
    level = 1
    while (2 << level) <= c:
        quad = jnp.right_shift(diff_bits, level) == 1
        offs = [jnp.where(quad, lm, 0.0).astype(jnp.bfloat16) for lm in lms]
        t16s = [t.astype(jnp.bfloat16) for t in ts]
        mos = [_dot(t16, off).astype(jnp.bfloat16) for t16, off in zip(t16s, offs)]
        ts = [t - _dot(mo, t16) for t, mo, t16 in zip(ts, mos, t16s)]
        level += 1
    for hh in heads:
        k = k_ref[:, cols[hh]].astype(jnp.float32)
        v = v_ref[:, cols[hh]].astype(jnp.float32)
        bcol = beta_col[:, hh:hh + 1]
        rhs = jnp.concatenate([v * bcol, k * (bcol * jnp.exp(gcum_col[:, hh:hh + 1]))], axis=1).astype(jnp.bfloat16)
        uw = _dot(ts[hh].astype(jnp.bfloat16), rhs)
        u_ref[:, cols[hh]] = uw[:, :DN_DK]
        wq_ref[hh, :c, :] = uw[:, DN_DK:].astype(wq_ref.dtype)


def _delta_scan_kernel(u_ref, wq_ref, akt_ref, dl_ref, z_ref, nw_ref, o_ref, state_ref):
    c = DELTA_CHUNK

    @pl.when(pl.program_id(0) == 0)
    def _():
        state_ref[...] = jnp.zeros(state_ref.shape, jnp.float32)

    nw = nw_ref[...]
    group = 4
    for h0 in range(0, DN_HEADS, group):
        hs = range(h0, h0 + group)
        cols = {h: slice(h * DN_DK, (h + 1) * DN_DK) for h in hs}
        wqs = {h: _dot(wq_ref[h], state_ref[h].astype(jnp.bfloat16)) for h in hs}
        rs = {h: _dot(akt_ref[h], (u_ref[:, cols[h]] - wqs[h][:c]).astype(jnp.bfloat16)) for h in hs}
        for h in hs:
            state_ref[h] = state_ref[h] * dl_ref[h] + rs[h][c:]
        for h in hs:
            z = z_ref[:, cols[h]]
            o_ref[:, cols[h]] = (_rms(wqs[h][c:] + rs[h][:c], nw) * (z * _sigmoid(z))).astype(o_ref.dtype)


def _delta(qkvn, p, small_t, a_log, dt_bias, norm_w):
    s = qkvn.shape[0]
    c = DELTA_CHUNK
    nc = s // c
    w = DELTA_HG * DN_DK
    nhg = DN_HEADS // DELTA_HG
    small_g = small_t.reshape(2 * nhg, DELTA_HG, s)
    a_log_g = a_log.reshape(nhg, DELTA_HG, 1)
    dt_bias_g = dt_bias.reshape(nhg, DELTA_HG, 1)

    def blk(off):
        return pl.BlockSpec((c, w), lambda i, j: (i, off + j))

    def head_rows(off):
        return pl.BlockSpec((None, DELTA_HG, c), lambda i, j: (off + j, 0, i))

    per_head = pl.BlockSpec((None, DELTA_HG, 1), lambda i, j: (j, 0, 0))
    stacked = pl.BlockSpec((None, DELTA_HG, 2 * c, DN_DK), lambda i, j: (i, j, 0, 0))
    u, wq, akt, dl = pl.pallas_call(
        _delta_local_kernel,
        grid=(nc, nhg),
        in_specs=[blk(0), blk(nhg), blk(2 * nhg), head_rows(0), head_rows(nhg), per_head, per_head],
        out_specs=[pl.BlockSpec((c, w), lambda i, j: (i, j)), stacked, stacked,
                   pl.BlockSpec((None, DELTA_HG, 1, DN_DK), lambda i, j: (i, j, 0, 0))],
        out_shape=[jax.ShapeDtypeStruct((s, DN_K_WIDTH), jnp.float32),
                   jax.ShapeDtypeStruct((nc, DN_HEADS, 2 * c, DN_DK), jnp.bfloat16),
                   jax.ShapeDtypeStruct((nc, DN_HEADS, 2 * c, DN_DK), jnp.bfloat16),
                   jax.ShapeDtypeStruct((nc, DN_HEADS, 1, DN_DK), jnp.float32)],
        compiler_params=_params(_ARB2),
        name="delta_local",
    )(qkvn, qkvn, qkvn, small_g, small_g, a_log_g, dt_bias_g)

    full = pl.BlockSpec((None, DN_HEADS, 2 * c, DN_DK), lambda i: (i, 0, 0, 0))
    return pl.pallas_call(
        _delta_scan_kernel,
        grid=(nc,),
        in_specs=[pl.BlockSpec((c, DN_K_WIDTH), lambda i: (i, 0)), full, full,
                  pl.BlockSpec((None, DN_HEADS, 1, DN_DK), lambda i: (i, 0, 0, 0)),
                  pl.BlockSpec((c, DN_K_WIDTH), lambda i: (i, P_DN_Z // DN_K_WIDTH)),
                  pl.BlockSpec((1, DN_DK), lambda i: (0, 0))],
        out_specs=pl.BlockSpec((c, DN_K_WIDTH), lambda i: (i, 0)),
        out_shape=jax.ShapeDtypeStruct((s, DN_K_WIDTH), jnp.bfloat16),
        scratch_shapes=[pltpu.VMEM((DN_HEADS, DN_DK, DN_DK), jnp.float32)],
        compiler_params=_params(_ARB1),
        name="delta_scan",
    )(u, wq, akt, dl, p, norm_w)


def _merge_kernel(o0, o1, o2, l0, l1, l2, yd_ref, woa_ref, wod_ref, ga_ref, gd_ref, out_ref, ya_ref):
    @pl.when(pl.program_id(1) == 0)
    def _():
        la, lb, lc = l0[...], l1[...], l2[...]
        m = jnp.maximum(jnp.maximum(la, lb), lc)
        ea, eb, ec = jnp.exp(la - m), jnp.exp(lb - m), jnp.exp(lc - m)
        ya = (o0[...] * ea + o1[...] * eb + o2[...] * ec) / (ea + eb + ec)
        ya_ref[...] = ya.astype(ya_ref.dtype)

    acc_a = _dot(ya_ref[...], woa_ref[...])
    acc_d = _dot(yd_ref[...], wod_ref[...])
    out_ref[...] = (_sigmoid(ga_ref[...]) * acc_a + _sigmoid(gd_ref[...]) * acc_d).astype(out_ref.dtype)


def _merge(os_, ls_, y_d, w_oa, w_od, p, tm=512, tn=1024):
    s = y_d.shape[0]
    n = w_oa.shape[1]
    att = pl.BlockSpec((tm, A_OUT_WIDTH), lambda i, j: (i, 0))
    gdb = P_GATE_D // tn
    return pl.pallas_call(
        _merge_kernel,
        grid=(s // tm, n // tn),
        in_specs=[att] * 6 + [
            pl.BlockSpec((tm, DN_K_WIDTH), lambda i, j: (i, 0)),
            pl.BlockSpec((A_OUT_WIDTH, tn), lambda i, j: (0, j)),
            pl.BlockSpec((DN_K_WIDTH, tn), lambda i, j: (0, j)),
            pl.BlockSpec((tm, tn), lambda i, j: (i, j)),
            pl.BlockSpec((tm, tn), lambda i, j: (i, gdb + j))],
        out_specs=pl.BlockSpec((tm, tn), lambda i, j: (i, j)),
        out_shape=jax.ShapeDtypeStruct((s, n), jnp.bfloat16),
        scratch_shapes=[pltpu.VMEM((tm, A_OUT_WIDTH), jnp.bfloat16)],
        compiler_params=_params(_ARB2),
        name="merge",
    )(*os_, *ls_, y_d, w_oa, w_od, p, p)


def _postmix_kernel(mix_ref, x_ref, gate_ref, pn_ref, fn_ref, shift_ref, scale_ref, x1_ref, h_ref):
    x1 = x_ref[...] + gate_ref[...] * _rms(mix_ref[...], pn_ref[...])
    x1_ref[...] = x1
    h_ref[...] = (_rms(x1, fn_ref[...]) * (1.0 + scale_ref[...]) + shift_ref[...]).astype(h_ref.dtype)


def _post_mix(mix, x, gate, post_w, ffn_w, shift, scale, tm=256):
    s, d = x.shape
    row = pl.BlockSpec((1, d), lambda i: (0, 0))
    blk = pl.BlockSpec((tm, d), lambda i: (i, 0))
    return pl.pallas_call(
        _postmix_kernel,
        grid=(s // tm,),
        in_specs=[blk, blk, row, row, row, row, row],
        out_specs=[blk, blk],
        out_shape=[jax.ShapeDtypeStruct((s, d), jnp.float32), jax.ShapeDtypeStruct((s, d), jnp.bfloat16)],
        compiler_params=_params(_ARB1),
        name="post_mix",
    )(mix, x, gate, post_w, ffn_w, shift, scale)


def _final_kernel(y_ref, x_ref, gate_ref, pn_ref, o_ref):
    o_ref[...] = x_ref[...] + gate_ref[...] * _rms(y_ref[...], pn_ref[...])


def _final(y, x1, gate, post_w, tm=256):
    s, d = x1.shape
    row = pl.BlockSpec((1, d), lambda i: (0, 0))
    blk = pl.BlockSpec((tm, d), lambda i: (i, 0))
    return pl.pallas_call(
        _final_kernel,
        grid=(s // tm,),
        in_specs=[blk, blk, row, row],
        out_specs=blk,
        out_shape=jax.ShapeDtypeStruct((s, d), jnp.float32),
        compiler_params=_params(_ARB1),
        name="final",
    )(y, x1, gate, post_w)


ROUTER_TM = 256


def _topk_desc(scores):
    work = scores
    vals = []
    for it in range(PEER_TOPK):
        mx = jnp.max(work, axis=0, keepdims=True)
        vals.append(mx)
        if it + 1 < PEER_TOPK:
            work = jnp.where(work == mx, -jnp.inf, work)
    return jnp.concatenate(vals, axis=0)


def _router_kernel(q_ref, k1_ref, k2_ref, s1_ref, e1_ref, s2_ref, e2_ref, thr_ref):
    k1 = k1_ref[...].astype(jnp.bfloat16)
    k2 = k2_ref[...].astype(jnp.bfloat16)

    def head(h, carry):
        c0 = pl.multiple_of(h * 2 * PEER_HALF, 2 * PEER_HALF)
        q1 = q_ref[:, pl.ds(c0, PEER_HALF)].astype(jnp.bfloat16)
        q2 = q_ref[:, pl.ds(c0 + PEER_HALF, PEER_HALF)].astype(jnp.bfloat16)
        s1 = _dot_nt(k1, q1)
        s2 = _dot_nt(k2, q2)
        v1 = _topk_desc(s1)
        v2 = _topk_desc(s2)
        cand = jnp.concatenate([v1[i:i + 1, :] + v2 for i in range(PEER_TOPK)], axis=0)
        thr = _topk_desc(cand)[PEER_TOPK - 1:PEER_TOPK, :]
        top = v1[0:1, :] + v2[0:1, :]
        zsum = jnp.sum(jnp.where(cand >= thr, jnp.exp(cand - top), 0.0), axis=0, keepdims=True)
        s1_ref[h] = s1
        e1_ref[h] = jnp.exp(s1 - v1[0:1, :]) / zsum
        s2_ref[h] = s2
        e2_ref[h] = jnp.exp(s2 - v2[0:1, :])
        thr_ref[pl.ds(h, 1), :] = thr
        return carry

    lax.fori_loop(0, PEER_HEADS, head, 0)


def _router(q, keys_1, keys_2):
    s = q.shape[0]
    tm = ROUTER_TM
    h_major = pl.BlockSpec((PEER_HEADS, PEER_KEYS, tm), lambda i: (0, 0, i))
    kspec = pl.BlockSpec((PEER_KEYS, PEER_HALF), lambda i: (0, 0))
    return pl.pallas_call(
        _router_kernel,
        grid=(s // tm,),
        in_specs=[pl.BlockSpec((tm, 2 * PEER_HALF * PEER_HEADS), lambda i: (i, 0)), kspec, kspec],
        out_specs=[h_major] * 4 + [pl.BlockSpec((PEER_HEADS, tm), lambda i: (0, i))],
        out_shape=[jax.ShapeDtypeStruct((PEER_HEADS, PEER_KEYS, s), jnp.float32)] * 4
        + [jax.ShapeDtypeStruct((PEER_HEADS, s), jnp.float32)],
        compiler_params=_params(_ARB1),
        name="router",
    )(q, keys_1, keys_2)


PEER_TM = 512
PEER_EC = 512
_A_PER_STEP = PEER_EC // PEER_KEYS


def _gelu(x):
    return 0.5 * x * (1.0 + lax.erf(x * (2.0 ** -0.5)))


_PEER_CHUNKS = PEER_EXPERTS // PEER_EC
_MXU_N = 256
_ACT_ROWS = 32


def _peer_kernel(h_ref, u_ref, v_ref, s1_ref, e1_ref, s2_ref, e2_ref, thr_ref, o_ref, sc0, sc1, act0, act1):
    step = pl.program_id(0)
    sc_ref, act_ref = (sc0, sc1), (act0, act1)

    @pl.when(step == 0)
    def _():
        for ref in sc_ref + act_ref:
            ref[...] = jnp.zeros(ref.shape, ref.dtype)

    @pl.when(jnp.maximum(step - 2, 0) % _PEER_CHUNKS == 0)
    def _():
        o_ref[...] = jnp.zeros(o_ref.shape, jnp.float32)

    tm, d_model = h_ref.shape

    def stages(cur, nxt):
        def value_sum(nt):
            dcol = slice(nt * _MXU_N, (nt + 1) * _MXU_N)
            o_ref[:, dcol] += _dot_tn(act_ref[cur][...], v_ref[:, dcol])

        def activations(al, lt, bs):
            tok = slice(lt * LANES, (lt + 1) * LANES)
            keys2 = slice(bs * _ACT_ROWS, (bs + 1) * _ACT_ROWS)
            gate = jnp.zeros((_ACT_ROWS, LANES), jnp.float32)
            for h in range(PEER_HEADS):
                cand = s1_ref[al, h:h + 1, tok] + s2_ref[h, keys2, tok]
                w = e1_ref[al, h:h + 1, tok] * e2_ref[h, keys2, tok]
                gate = gate + jnp.where(cand >= thr_ref[h:h + 1, tok], w, 0.0)
            ex = slice(al * PEER_KEYS + bs * _ACT_ROWS, al * PEER_KEYS + (bs + 1) * _ACT_ROWS)
            act_ref[nxt][ex, tok] = (gate * _gelu(sc_ref[cur][ex, tok])).astype(act_ref[nxt].dtype)

        def scores(nt):
            tok = slice(nt * _MXU_N, (nt + 1) * _MXU_N)
            sc_ref[nxt][:, tok] = _dot_nt(u_ref[...], h_ref[tok, :])

        s3 = [functools.partial(value_sum, nt) for nt in range(d_model // _MXU_N)]
        s2 = [functools.partial(activations, al, lt, bs) for al in range(_A_PER_STEP)
              for lt in range(tm // LANES) for bs in range(PEER_KEYS // _ACT_ROWS)]
        s1 = [functools.partial(scores, nt) for nt in range(tm // _MXU_N)]
        rounds = max(len(s3), len(s2))
        for k in range(rounds):
            for pieces in (s3, s2, s1):
                lo, hi = k * len(pieces) // rounds, (k + 1) * len(pieces) // rounds
                for piece in pieces[lo:hi]:
                    piece()

    @pl.when(step % 2 == 0)
    def _():
        stages(0, 1)

    @pl.when(step % 2 == 1)
    def _():
        stages(1, 0)


def _peer(h2, u16, v16, s1, e1, s2, e2, thr):
    s, d = h2.shape
    tm, ec = PEER_TM, PEER_EC
    npairs = (s // tm) * _PEER_CHUNKS

    def pair(step, lag):
        idx = jnp.clip(step - lag, 0, npairs - 1)
        return idx // _PEER_CHUNKS, idx % _PEER_CHUNKS

    def tile(lag):
        return lambda t: (pair(t, lag)[0], 0)

    def chunk(lag):
        return lambda t: (pair(t, lag)[1], 0)

    a_major = pl.BlockSpec((_A_PER_STEP, PEER_HEADS, tm), lambda t: (pair(t, 1)[1], 0, pair(t, 1)[0]))
    h_major = pl.BlockSpec((PEER_HEADS, PEER_KEYS, tm), lambda t: (0, 0, pair(t, 1)[0]))
    return pl.pallas_call(
        _peer_kernel,
        grid=(npairs + 2,),
        in_specs=[pl.BlockSpec((tm, d), tile(0)),
                  pl.BlockSpec((ec, d), chunk(0)),
                  pl.BlockSpec((ec, d), chunk(2)),
                  a_major, a_major, h_major, h_major,
                  pl.BlockSpec((PEER_HEADS, tm), lambda t: (0, pair(t, 1)[0]))],
        out_specs=pl.BlockSpec((tm, d), tile(2)),
        out_shape=jax.ShapeDtypeStruct((s, d), jnp.float32),
        scratch_shapes=[pltpu.VMEM((ec, tm), jnp.float32)] * 2 + [pltpu.VMEM((ec, tm), jnp.bfloat16)] * 2,
        compiler_params=_params(_ARB1),
        name="peer",
    )(h2, u16, v16, s1, e1, s2, e2, thr)


def _rope_tables(positions):
    half = HEAD_DIM // 2
    inv_freq = ROPE_THETA ** (-jnp.arange(half, dtype=jnp.float32) / half)
    ang = positions.astype(jnp.float32)[:, None] * inv_freq[None, :]
    cos, sin = jnp.cos(ang), jnp.sin(ang)
    return jnp.concatenate([cos, cos], axis=-1), jnp.concatenate([-sin, sin], axis=-1)


def _layer(x, c_col, positions, w_ada, b_ada, attn_pre_norm, attn_post_norm, w_in, conv_w, a_log, dt_bias,
           dn_norm_w, w_o_attn, w_o_dn, w_out, ffn_pre_norm, ffn_post_norm, w_peer_q, peer_keys_1,
           peer_keys_2, expert_u, expert_v):
    d = D_MODEL
    bf = jnp.bfloat16
    mod = _adaln(c_col, w_ada, b_ada[None, :])
    shift_mix, scale_mix, gate_mix, shift_ffn, scale_ffn, gate_ffn = [mod[:, i * d:(i + 1) * d] for i in range(6)]

    h = _prenorm(x, attn_pre_norm[None, :], shift_mix, scale_mix)
    w_main = jnp.concatenate([w_in[:, W_GATES:], w_in[:, W_DN:W_SMALL], w_in[:, W_ATTN:W_DN]], axis=1).astype(bf)
    p = _matmul(h, w_main, jnp.float32, 1024, 512, "in_proj")
    small_t = _small_proj(h, w_in[:, W_SMALL:W_GATES].T.astype(bf))

    cos, sin = _rope_tables(positions)
    outs, lses = [], []
    for g, (_, dilation) in enumerate(DILATED_GROUPS):
        o, lse = _attention_group(p, cos, sin, g, dilation)
        outs.append(o)
        lses.append(lse)

    qkvn = _dn_prep(p, conv_w)
    y_d = _delta(qkvn, p, small_t, a_log, dt_bias, dn_norm_w[None, :])

    merged = _merge(outs, lses, y_d, w_o_attn.astype(bf), w_o_dn.astype(bf), p)
    mix = _matmul(merged, w_out.astype(bf), jnp.float32, 1024, 512, "out_proj")
    x1, h2 = _post_mix(mix, x, gate_mix, attn_post_norm[None, :], ffn_pre_norm[None, :], shift_ffn, scale_ffn)

    q = _matmul(h2, w_peer_q.astype(bf), jnp.float32, 1024, 512, "peer_q")
    s1, e1, s2, e2, thr = _router(q, peer_keys_1, peer_keys_2)
    s1, e1 = jnp.transpose(s1, (1, 0, 2)), jnp.transpose(e1, (1, 0, 2))
    y = _peer(h2, expert_u.astype(bf), expert_v.astype(bf), s1, e1, s2, e2, thr)
    return _final(y, x1, gate_ffn, ffn_post_norm[None, :])


def kernel(x, c, positions, w_ada, b_ada, attn_pre_norm, attn_post_norm, w_in, conv_w, a_log, dt_bias, dn_norm_w,
           w_o_attn, w_o_dn, w_out, ffn_pre_norm, ffn_post_norm, w_peer_q, peer_keys_1, peer_keys_2, expert_u,
           expert_v):
    batch, seq, d = x.shape
    depth = w_ada.shape[0]
    outs = []
    for b in range(batch):
        xb = x[b]
        c_col = c[b][:, None]
        for l in range(depth):
            xb = _layer(xb, c_col, positions[b], w_ada[l], b_ada[l], attn_pre_norm[l], attn_post_norm[l], w_in[l],
                        conv_w[l], a_log[l], dt_bias[l], dn_norm_w[l], w_o_attn[l], w_o_dn[l], w_out[l],
                        ffn_pre_norm[l], ffn_post_norm[l], w_peer_q[l], peer_keys_1[l], peer_keys_2[l],
                        expert_u[l], expert_v[l])
        outs.append(xb)
    return jnp.stack(outs, axis=0)
```

```python
import functools

import jax
import jax.numpy as jnp
from jax import lax
from jax.experimental import pallas as pl
from jax.experimental.pallas import tpu as pltpu

D_MODEL = 4096
NORM_EPS = 1e-6
NEG_INF = -1e30

HEAD_DIM = 128
ROPE_THETA = 10000.0
DILATED_GROUPS = ((128, 1), (512, 4), (2048, 16))
HEADS_PER_GROUP = 4
A_WIDTH = 1536
A_OUT_WIDTH = HEADS_PER_GROUP * HEAD_DIM
ATTN_BLOCK = 128
ATTN_SPAN = 2048

DN_HEADS = 16
DN_DK = 128
DN_CONV = 4
DN_CHUNK = 64
DN_K_WIDTH = DN_HEADS * DN_DK
DN_QKV_WIDTH = 3 * DN_K_WIDTH

PEER_HEADS = 8
PEER_KEYS = 128
PEER_EXPERTS = PEER_KEYS * PEER_KEYS
PEER_HALF = 128
PEER_TOPK = 16

P_ATTN_Q = 0
P_ATTN_K = A_WIDTH
P_ATTN_V = 2 * A_WIDTH
P_DN_QKV = 0
P_DN_Z = DN_QKV_WIDTH
P_GATE_A = 0
P_GATE_D = D_MODEL
W_ATTN = 0
W_DN = 3 * A_WIDTH
W_Z = W_DN + DN_QKV_WIDTH
W_SMALL = W_Z + DN_K_WIDTH
W_GATES = W_SMALL + 2 * DN_HEADS

LANES = 128
VMEM_LIMIT = 52 * 1024 * 1024

_ARB1 = ("arbitrary",)
_ARB2 = ("arbitrary", "arbitrary")


def _params(sem):
    return pltpu.CompilerParams(dimension_semantics=sem, vmem_limit_bytes=VMEM_LIMIT)


def _sigmoid(x):
    return 1.0 / (1.0 + jnp.exp(-x))


def _dot(a, b):
    return jnp.dot(a, b, preferred_element_type=jnp.float32)


def _dot_nt(a, b):
    return lax.dot_general(a, b, (((1,), (1,)), ((), ())), preferred_element_type=jnp.float32)


def _dot_tn(a, b):
    return lax.dot_general(a, b, (((0,), (0,)), ((), ())), preferred_element_type=jnp.float32)


def _rms(x, w):
    return x * lax.rsqrt(jnp.mean(x * x, axis=-1, keepdims=True) + NORM_EPS) * w


def _adaln_kernel(c_ref, w_ref, b_ref, o_ref):
    c = c_ref[...]
    act = c * _sigmoid(c)
    o_ref[...] = jnp.sum(act * w_ref[...], axis=0, keepdims=True) + b_ref[...]


def _adaln(c_col, w, b, tn=512):
    d, n = w.shape
    return pl.pallas_call(
        _adaln_kernel,
        grid=(n // tn,),
        in_specs=[pl.BlockSpec((d, 1), lambda j: (0, 0)),
                  pl.BlockSpec((d, tn), lambda j: (0, j)),
                  pl.BlockSpec((1, tn), lambda j: (0, j))],
        out_specs=pl.BlockSpec((1, tn), lambda j: (0, j)),
        out_shape=jax.ShapeDtypeStruct((1, n), jnp.float32),
        compiler_params=_params(_ARB1),
        name="adaln",
    )(c_col, w, b)


def _prenorm_kernel(x_ref, nw_ref, shift_ref, scale_ref, o_ref):
    y = _rms(x_ref[...], nw_ref[...])
    o_ref[...] = (y * (1.0 + scale_ref[...]) + shift_ref[...]).astype(o_ref.dtype)


def _prenorm(x, nw, shift, scale, tm=256):
    s, d = x.shape
    row = pl.BlockSpec((1, d), lambda i: (0, 0))
    return pl.pallas_call(
        _prenorm_kernel,
        grid=(s // tm,),
        in_specs=[pl.BlockSpec((tm, d), lambda i: (i, 0)), row, row, row],
        out_specs=pl.BlockSpec((tm, d), lambda i: (i, 0)),
        out_shape=jax.ShapeDtypeStruct((s, d), jnp.bfloat16),
        compiler_params=_params(_ARB1),
        name="prenorm",
    )(x, nw, shift, scale)


def _mm_kernel(a_ref, b_ref, o_ref):
    o_ref[...] = _dot(a_ref[...], b_ref[...]).astype(o_ref.dtype)


def _matmul(a, b, out_dtype, tm, tn, name):
    m, k = a.shape
    n = b.shape[1]
    return pl.pallas_call(
        _mm_kernel,
        grid=(m // tm, n // tn),
        in_specs=[pl.BlockSpec((tm, k), lambda i, j: (i, 0)),
                  pl.BlockSpec((k, tn), lambda i, j: (0, j))],
        out_specs=pl.BlockSpec((tm, tn), lambda i, j: (i, j)),
        out_shape=jax.ShapeDtypeStruct((m, n), out_dtype),
        compiler_params=_params(_ARB2),
        name=name,
    )(a, b)


def _matmul_cols(a, b, col0, n, out_dtype, tm, tn, name):
    m, k = a.shape
    cb0 = col0 // tn
    return pl.pallas_call(
        _mm_kernel,
        grid=(m // tm, n // tn),
        in_specs=[pl.BlockSpec((tm, k), lambda i, j: (i, 0)),
                  pl.BlockSpec((k, tn), lambda i, j: (0, cb0 + j))],
        out_specs=pl.BlockSpec((tm, tn), lambda i, j: (i, j)),
        out_shape=jax.ShapeDtypeStruct((m, n), out_dtype),
        compiler_params=_params(_ARB2),
        name=name,
    )(a, b)


def _small_kernel(h_ref, w_ref, ot_ref):
    ot_ref[...] = _dot(h_ref[...], w_ref[...]).T[:ot_ref.shape[0], :]


def _small_proj(h, w_pad, tm=512):
    s, d = h.shape
    r = 2 * DN_HEADS
    return pl.pallas_call(
        _small_kernel,
        grid=(s // tm,),
        in_specs=[pl.BlockSpec((tm, d), lambda i: (i, 0)),
                  pl.BlockSpec((d, LANES), lambda i: (0, 0))],
        out_specs=pl.BlockSpec((r, tm), lambda i: (0, i)),
        out_shape=jax.ShapeDtypeStruct((r, s), jnp.float32),
        compiler_params=_params(_ARB1),
        name="small_proj",
    )(h, w_pad)


def _rope(x, cos, sin):
    return x * cos + pltpu.roll(x, HEAD_DIM // 2, 1) * sin


def _attn_kernel(qc_ref, kc_ref, vc_ref, kp_ref, vp_ref, cc_ref, sc_ref, cp_ref, sp_ref, o_ref, l_ref, *, dilation):
    n = pl.program_id(0)
    sub = ATTN_BLOCK * dilation
    qi = lax.broadcasted_iota(jnp.int32, (ATTN_BLOCK, ATTN_BLOCK), 0)
    kj = lax.broadcasted_iota(jnp.int32, (ATTN_BLOCK, ATTN_BLOCK), 1)
    mask_c = kj <= qi
    mask_in = kj >= qi
    mask_first = jnp.logical_and(mask_in, n > 0)
    scale = HEAD_DIM ** -0.5

    def rows_of(sb, r):
        return pl.ds(sb * sub + r, ATTN_BLOCK, stride=dilation) if dilation > 1 else pl.ds(sb * sub, ATTN_BLOCK)

    def scores(sb, r):
        rows = rows_of(sb, r)
        cos_c, sin_c = cc_ref[rows, :], sc_ref[rows, :]
        if sb == 0:
            prow = rows_of(0, r)
            k_prev, v_prev, cos_p, sin_p, mask_p = kp_ref[prow, :], vp_ref[prow, :], cp_ref[prow, :], sp_ref[prow, :], mask_first
        else:
            prow = rows_of(sb - 1, r)
            k_prev, v_prev, cos_p, sin_p, mask_p = kc_ref[prow, :], vc_ref[prow, :], cc_ref[prow, :], sc_ref[prow, :], mask_in
        q = _rope(qc_ref[rows, :], cos_c, sin_c).astype(jnp.bfloat16)
        k_c = _rope(kc_ref[rows, :], cos_c, sin_c).astype(jnp.bfloat16)
        k_p = _rope(k_prev, cos_p, sin_p).astype(jnp.bfloat16)
        s_c = jnp.where(mask_c, _dot_nt(q, k_c) * scale, NEG_INF)
        s_p = jnp.where(mask_p, _dot_nt(q, k_p) * scale, NEG_INF)
        return rows, s_c, s_p, v_prev

    def finish(rows, s_c, s_p, v_prev):
        m = jnp.maximum(jnp.max(s_c, axis=-1, keepdims=True), jnp.max(s_p, axis=-1, keepdims=True))
        p_c = jnp.exp(s_c - m)
        p_p = jnp.exp(s_p - m)
        den = jnp.sum(p_c, axis=-1, keepdims=True) + jnp.sum(p_p, axis=-1, keepdims=True)
        pv = (_dot(p_c.astype(jnp.bfloat16), vc_ref[rows, :].astype(jnp.bfloat16))
              + _dot(p_p.astype(jnp.bfloat16), v_prev.astype(jnp.bfloat16)))
        o_ref[rows, :] = pv / den
        l_ref[rows, :] = jnp.broadcast_to(m + jnp.log(den), (ATTN_BLOCK, HEAD_DIM))

    items = [(sb, r) for sb in range(ATTN_SPAN // sub) for r in range(dilation)]
    batch = 4
    for b0 in range(0, len(items), batch):
        staged = [scores(sb, r) for sb, r in items[b0:b0 + batch]]
        for st in staged:
            finish(*st)


def _attention_group(p, cos, sin, g, dilation):
    s = p.shape[0]
    span = ATTN_SPAN
    sub = ATTN_BLOCK * dilation
    per = span // sub
    qb, kb, vb = [(off + g * A_OUT_WIDTH) // HEAD_DIM for off in (P_ATTN_Q, P_ATTN_K, P_ATTN_V)]

    def cur(off):
        return pl.BlockSpec((span, HEAD_DIM), lambda n, h: (n, off + h))

    def prev(off):
        return pl.BlockSpec((sub, HEAD_DIM), lambda n, h: (jnp.maximum(n * per - 1, 0), off + h))

    tab_c = pl.BlockSpec((span, HEAD_DIM), lambda n, h: (n, 0))
    tab_p = pl.BlockSpec((sub, HEAD_DIM), lambda n, h: (jnp.maximum(n * per - 1, 0), 0))
    out = pl.BlockSpec((span, HEAD_DIM), lambda n, h: (n, h))
    return pl.pallas_call(
        functools.partial(_attn_kernel, dilation=dilation),
        grid=(s // span, HEADS_PER_GROUP),
        in_specs=[cur(qb), cur(kb), cur(vb), prev(kb), prev(vb), tab_c, tab_c, tab_p, tab_p],
        out_specs=[out, out],
        out_shape=[jax.ShapeDtypeStruct((s, A_OUT_WIDTH), jnp.float32)] * 2,
        compiler_params=_params(_ARB2),
        name=f"attn_d{dilation}",
    )(p, p, p, p, p, cos, sin, cos, sin)


DN_PREP_COLS = 512
DN_HALO = 16
_QK_BLOCKS = 2 * DN_K_WIDTH // DN_PREP_COLS


def _dnprep_kernel(x_ref, halo_ref, w_ref, o_ref):
    i = pl.program_id(0)
    j = pl.program_id(1)
    x = x_ref[...].astype(jnp.float32)
    tm = x.shape[0]
    halo = jnp.where(i > 0, halo_ref[...].astype(jnp.float32), 0.0)
    xx = jnp.concatenate([halo, x], axis=0)
    w = w_ref[...]
    acc = x * w[DN_CONV - 1:DN_CONV, :]
    for sft in range(1, DN_CONV):
        acc = acc + xx[DN_HALO - sft:DN_HALO - sft + tm, :] * w[DN_CONV - 1 - sft:DN_CONV - sft, :]
    y = acc * _sigmoid(acc)
    for hh in range(DN_PREP_COLS // DN_DK):
        sl = slice(hh * DN_DK, (hh + 1) * DN_DK)
        yh = y[:, sl]
        f = lax.rsqrt(jnp.sum(yh * yh, axis=-1, keepdims=True) + NORM_EPS)
        f = jnp.where(j < _QK_BLOCKS, f, 1.0)
        o_ref[:, sl] = (yh * f).astype(o_ref.dtype)


def _dn_prep(p, conv_w, tm=512):
    s = p.shape[0]
    cb0 = P_DN_QKV // DN_PREP_COLS
    return pl.pallas_call(
        _dnprep_kernel,
        grid=(s // tm, DN_QKV_WIDTH // DN_PREP_COLS),
        in_specs=[pl.BlockSpec((tm, DN_PREP_COLS), lambda i, j: (i, cb0 + j)),
                  pl.BlockSpec((DN_HALO, DN_PREP_COLS),
                               lambda i, j: (jnp.maximum(i * (tm // DN_HALO) - 1, 0), cb0 + j)),
                  pl.BlockSpec((DN_CONV, DN_PREP_COLS), lambda i, j: (0, j))],
        out_specs=pl.BlockSpec((tm, DN_PREP_COLS), lambda i, j: (i, j)),
        out_shape=jax.ShapeDtypeStruct((s, DN_QKV_WIDTH), jnp.bfloat16),
        compiler_params=_params(_ARB2),
        name="dn_prep",
    )(p, p, conv_w)


DELTA_CHUNK = 128
DELTA_HG = 8


def _softplus(x):
    return jnp.maximum(x, 0.0) + jnp.log1p(jnp.exp(-jnp.abs(x)))


def _delta_local_kernel(q_ref, k_ref, v_ref, braw_ref, araw_ref, alog_ref, dtb_ref, u_ref, wq_ref, akt_ref, dl_ref):
    c = DELTA_CHUNK
    hi = lax.Precision.HIGHEST
    beta_t = _sigmoid(braw_ref[...])
    g_t = -jnp.exp(alog_ref[...]) * _softplus(araw_ref[...] + dtb_ref[...])

    ii = lax.broadcasted_iota(jnp.int32, (c, c), 0)
    jj = lax.broadcasted_iota(jnp.int32, (c, c), 1)
    lower = (ii >= jj).astype(jnp.float32)
    upper = (ii <= jj).astype(jnp.float32)
    eye = (ii == jj).astype(jnp.float32)
    scale = DN_DK ** -0.5

    gcum_row = jnp.dot(g_t, upper, precision=hi, preferred_element_type=jnp.float32)
    gcum_col = lax.dot_general(lower, g_t, (((1,), (1,)), ((), ())), precision=hi,
                               preferred_element_type=jnp.float32)
    beta_col = lax.dot_general(eye, beta_t, (((1,), (1,)), ((), ())), precision=hi,
                               preferred_element_type=jnp.float32)
    heads = range(DELTA_HG)
    cols = [slice(hh * DN_DK, (hh + 1) * DN_DK) for hh in heads]
    lms = []
    for hh in heads:
        q = q_ref[:, cols[hh]].astype(jnp.float32)
        k = k_ref[:, cols[hh]].astype(jnp.float32)
        gcol = gcum_col[:, hh:hh + 1]
        dec = jnp.exp(jnp.minimum(gcol - gcum_row[hh:hh + 1, :], 0.0))
        both = _dot_nt(jnp.concatenate([q, k], axis=0).astype(jnp.bfloat16), k.astype(jnp.bfloat16))
        g_last = gcol[c - 1:c, :]
        akt_ref[hh, :c, :] = (jnp.where(ii >= jj, both[:c] * dec, 0.0) * scale).astype(akt_ref.dtype)
        akt_ref[hh, c:, :] = (k * jnp.exp(g_last - gcol)).T.astype(akt_ref.dtype)
        wq_ref[hh, c:, :] = (q * (scale * jnp.exp(gcol))).astype(wq_ref.dtype)
        dl_ref[hh] = jnp.broadcast_to(jnp.exp(g_last), (1, DN_DK))
        lms.append(jnp.where(ii > jj, both[c:] * dec * beta_col[:, hh:hh + 1], 0.0))
    diff_bits = jnp.bitwise_xor(ii, jj)
    ts = [eye - jnp.where(diff_bits == 1, lm, 0.0) for lm in lms]
    level = 1
    while (2 << level) <= c:
        quad = jnp.right_shift(diff_bits, level) == 1
        offs = [jnp.where(quad, lm, 0.0).astype(jnp.bfloat16) for lm in lms]
        t16s = [t.astype(jnp.bfloat16) for t in ts]
        mos = [_dot(t16, off).astype(jnp.bfloat16) for t16, off in zip(t16s, offs)]
        ts = [t - _dot(mo, t16) for t, mo, t16 in zip(ts, mos, t16s)]
        level += 1
    for hh in heads:
        k = k_ref[:, cols[hh]].astype(jnp.float32)
        v = v_ref[:, cols[hh]].astype(jnp.float32)
        bcol = beta_col[:, hh:hh + 1]
        rhs = jnp.concatenate([v * bcol, k * (bcol * jnp.exp(gcum_col[:, hh:hh + 1]))], axis=1).astype(jnp.bfloat16)
        uw = _dot(ts[hh].astype(jnp.bfloat16), rhs)
        u_ref[:, cols[hh]] = uw[:, :DN_DK]
        wq_ref[hh, :c, :] = uw[:, DN_DK:].astype(wq_ref.dtype)


def _delta_scan_kernel(u_ref, wq_ref, akt_ref, dl_ref, z_ref, nw_ref, o_ref, state_ref):
    c = DELTA_CHUNK

    @pl.when(pl.program_id(0) == 0)
    def _():
        state_ref[...] = jnp.zeros(state_ref.shape, jnp.float32)

    nw = nw_ref[...]
    group = 4
    for h0 in range(0, DN_HEADS, group):
        hs = range(h0, h0 + group)
        cols = {h: slice(h * DN_DK, (h + 1) * DN_DK) for h in hs}
        wqs = {h: _dot(wq_ref[h], state_ref[h].astype(jnp.bfloat16)) for h in hs}
        rs = {h: _dot(akt_ref[h], (u_ref[:, cols[h]] - wqs[h][:c]).astype(jnp.bfloat16)) for h in hs}
        for h in hs:
            state_ref[h] = state_ref[h] * dl_ref[h] + rs[h][c:]
        for h in hs:
            z = z_ref[:, cols[h]].astype(jnp.float32)
            o_ref[:, cols[h]] = (_rms(wqs[h][c:] + rs[h][:c], nw) * (z * _sigmoid(z))).astype(o_ref.dtype)


def _delta(qkvn, p, small_t, a_log, dt_bias, norm_w):
    s = qkvn.shape[0]
    c = DELTA_CHUNK
    nc = s // c
    w = DELTA_HG * DN_DK
    nhg = DN_HEADS // DELTA_HG
    small_g = small_t.reshape(2 * nhg, DELTA_HG, s)
    a_log_g = a_log.reshape(nhg, DELTA_HG, 1)
    dt_bias_g = dt_bias.reshape(nhg, DELTA_HG, 1)

    def blk(off):
        return pl.BlockSpec((c, w), lambda i, j: (i, off + j))

    def head_rows(off):
        return pl.BlockSpec((None, DELTA_HG, c), lambda i, j: (off + j, 0, i))

    per_head = pl.BlockSpec((None, DELTA_HG, 1), lambda i, j: (j, 0, 0))
    stacked = pl.BlockSpec((None, DELTA_HG, 2 * c, DN_DK), lambda i, j: (i, j, 0, 0))
    u, wq, akt, dl = pl.pallas_call(
        _delta_local_kernel,
        grid=(nc, nhg),
        in_specs=[blk(0), blk(nhg), blk(2 * nhg), head_rows(0), head_rows(nhg), per_head, per_head],
        out_specs=[pl.BlockSpec((c, w), lambda i, j: (i, j)), stacked, stacked,
                   pl.BlockSpec((None, DELTA_HG, 1, DN_DK), lambda i, j: (i, j, 0, 0))],
        out_shape=[jax.ShapeDtypeStruct((s, DN_K_WIDTH), jnp.float32),
                   jax.ShapeDtypeStruct((nc, DN_HEADS, 2 * c, DN_DK), jnp.bfloat16),
                   jax.ShapeDtypeStruct((nc, DN_HEADS, 2 * c, DN_DK), jnp.bfloat16),
                   jax.ShapeDtypeStruct((nc, DN_HEADS, 1, DN_DK), jnp.float32)],
        compiler_params=_params(_ARB2),
        name="delta_local",
    )(qkvn, qkvn, qkvn, small_g, small_g, a_log_g, dt_bias_g)

    full = pl.BlockSpec((None, DN_HEADS, 2 * c, DN_DK), lambda i: (i, 0, 0, 0))
    return pl.pallas_call(
        _delta_scan_kernel,
        grid=(nc,),
        in_specs=[pl.BlockSpec((c, DN_K_WIDTH), lambda i: (i, 0)), full, full,
                  pl.BlockSpec((None, DN_HEADS, 1, DN_DK), lambda i: (i, 0, 0, 0)),
                  pl.BlockSpec((c, DN_K_WIDTH), lambda i: (i, P_DN_Z // DN_K_WIDTH)),
                  pl.BlockSpec((1, DN_DK), lambda i: (0, 0))],
        out_specs=pl.BlockSpec((c, DN_K_WIDTH), lambda i: (i, 0)),
        out_shape=jax.ShapeDtypeStruct((s, DN_K_WIDTH), jnp.bfloat16),
        scratch_shapes=[pltpu.VMEM((DN_HEADS, DN_DK, DN_DK), jnp.float32)],
        compiler_params=_params(_ARB1),
        name="delta_scan",
    )(u, wq, akt, dl, p, norm_w)


def _merge_kernel(o0, o1, o2, l0, l1, l2, yd_ref, woa_ref, wod_ref, ga_ref, gd_ref, out_ref, ya_ref):
    @pl.when(pl.program_id(1) == 0)
    def _():
        la, lb, lc = l0[...], l1[...], l2[...]
        m = jnp.maximum(jnp.maximum(la, lb), lc)
        ea, eb, ec = jnp.exp(la - m), jnp.exp(lb - m), jnp.exp(lc - m)
        ya = (o0[...] * ea + o1[...] * eb + o2[...] * ec) / (ea + eb + ec)
        ya_ref[...] = ya.astype(ya_ref.dtype)

    acc_a = _dot(ya_ref[...], woa_ref[...])
    acc_d = _dot(yd_ref[...], wod_ref[...])
    gate_a = _sigmoid(ga_ref[...].astype(jnp.float32))
    gate_d = _sigmoid(gd_ref[...].astype(jnp.float32))
    out_ref[...] = (gate_a * acc_a + gate_d * acc_d).astype(out_ref.dtype)


def _merge(os_, ls_, y_d, w_oa, w_od, p, tm=512, tn=1024):
    s = y_d.shape[0]
    n = w_oa.shape[1]
    att = pl.BlockSpec((tm, A_OUT_WIDTH), lambda i, j: (i, 0))
    gdb = P_GATE_D // tn
    return pl.pallas_call(
        _merge_kernel,
        grid=(s // tm, n // tn),
        in_specs=[att] * 6 + [
            pl.BlockSpec((tm, DN_K_WIDTH), lambda i, j: (i, 0)),
            pl.BlockSpec((A_OUT_WIDTH, tn), lambda i, j: (0, j)),
            pl.BlockSpec((DN_K_WIDTH, tn), lambda i, j: (0, j)),
            pl.BlockSpec((tm, tn), lambda i, j: (i, j)),
            pl.BlockSpec((tm, tn), lambda i, j: (i, gdb + j))],
        out_specs=pl.BlockSpec((tm, tn), lambda i, j: (i, j)),
        out_shape=jax.ShapeDtypeStruct((s, n), jnp.bfloat16),
        scratch_shapes=[pltpu.VMEM((tm, A_OUT_WIDTH), jnp.bfloat16)],
        compiler_params=_params(_ARB2),
        name="merge",
    )(*os_, *ls_, y_d, w_oa, w_od, p, p)


def _postmix_kernel(mix_ref, x_ref, gate_ref, pn_ref, fn_ref, shift_ref, scale_ref, x1_ref, h_ref):
    x1 = x_ref[...] + gate_ref[...] * _rms(mix_ref[...], pn_ref[...])
    x1_ref[...] = x1
    h_ref[...] = (_rms(x1, fn_ref[...]) * (1.0 + scale_ref[...]) + shift_ref[...]).astype(h_ref.dtype)


def _post_mix(mix, x, gate, post_w, ffn_w, shift, scale, tm=256):
    s, d = x.shape
    row = pl.BlockSpec((1, d), lambda i: (0, 0))
    blk = pl.BlockSpec((tm, d), lambda i: (i, 0))
    return pl.pallas_call(
        _postmix_kernel,
        grid=(s // tm,),
        in_specs=[blk, blk, row, row, row, row, row],
        out_specs=[blk, blk],
        out_shape=[jax.ShapeDtypeStruct((s, d), jnp.float32), jax.ShapeDtypeStruct((s, d), jnp.bfloat16)],
        compiler_params=_params(_ARB1),
        name="post_mix",
    )(mix, x, gate, post_w, ffn_w, shift, scale)


def _final_kernel(y_ref, x_ref, gate_ref, pn_ref, o_ref):
    o_ref[...] = x_ref[...] + gate_ref[...] * _rms(y_ref[...], pn_ref[...])


def _final(y, x1, gate, post_w, tm=256):
    s, d = x1.shape
    row = pl.BlockSpec((1, d), lambda i: (0, 0))
    blk = pl.BlockSpec((tm, d), lambda i: (i, 0))
    return pl.pallas_call(
        _final_kernel,
        grid=(s // tm,),
        in_specs=[blk, blk, row, row],
        out_specs=blk,
        out_shape=jax.ShapeDtypeStruct((s, d), jnp.float32),
        compiler_params=_params(_ARB1),
        name="final",
    )(y, x1, gate, post_w)


ROUTER_TM = 256


def _topk_desc(scores):
    work = scores
    vals = []
    for it in range(PEER_TOPK):
        mx = jnp.max(work, axis=0, keepdims=True)
        vals.append(mx)
        if it + 1 < PEER_TOPK:
            work = jnp.where(work == mx, -jnp.inf, work)
    return jnp.concatenate(vals, axis=0)


def _candidate_sums(v1, v2):
    half = PEER_TOPK // 2
    row = lax.broadcasted_iota(jnp.int32, (half, v1.shape[1]), 0)
    parts = [v1[0:1, :] + v2]
    for i in range(1, half):
        parts.append(jnp.where(row < PEER_TOPK // (i + 1), v1[i:i + 1, :] + v2[:half, :], -jnp.inf))
    parts.append(v1[half:, :] + v2[0:1, :])
    return jnp.concatenate(parts, axis=0)


def _router_kernel(q_ref, k1_ref, k2_ref, s1_ref, e1_ref, s2_ref, e2_ref, thr_ref):
    k1 = k1_ref[...].astype(jnp.bfloat16)
    k2 = k2_ref[...].astype(jnp.bfloat16)

    def head(h, carry):
        c0 = pl.multiple_of(h * 2 * PEER_HALF, 2 * PEER_HALF)
        q1 = q_ref[:, pl.ds(c0, PEER_HALF)].astype(jnp.bfloat16)
        q2 = q_ref[:, pl.ds(c0 + PEER_HALF, PEER_HALF)].astype(jnp.bfloat16)
        s1 = _dot_nt(k1, q1)
        s2 = _dot_nt(k2, q2)
        v1 = _topk_desc(s1)
        v2 = _topk_desc(s2)
        cand = _candidate_sums(v1, v2)
        thr = _topk_desc(cand)[PEER_TOPK - 1:PEER_TOPK, :]
        top = v1[0:1, :] + v2[0:1, :]
        zsum = jnp.sum(jnp.where(cand >= thr, jnp.exp(cand - top), 0.0), axis=0, keepdims=True)
        s1_ref[h] = s1
        e1_ref[h] = jnp.exp(s1 - v1[0:1, :]) / zsum
        s2_ref[h] = s2
        e2_ref[h] = jnp.exp(s2 - v2[0:1, :])
        thr_ref[pl.ds(h, 1), :] = thr
        return carry

    lax.fori_loop(0, PEER_HEADS, head, 0)


def _router(q, keys_1, keys_2):
    s = q.shape[0]
    tm = ROUTER_TM
    h_major = pl.BlockSpec((PEER_HEADS, PEER_KEYS, tm), lambda i: (0, 0, i))
    kspec = pl.BlockSpec((PEER_KEYS, PEER_HALF), lambda i: (0, 0))
    return pl.pallas_call(
        _router_kernel,
        grid=(s // tm,),
        in_specs=[pl.BlockSpec((tm, 2 * PEER_HALF * PEER_HEADS), lambda i: (i, 0)), kspec, kspec],
        out_specs=[h_major] * 4 + [pl.BlockSpec((PEER_HEADS, tm), lambda i: (0, i))],
        out_shape=[jax.ShapeDtypeStruct((PEER_HEADS, PEER_KEYS, s), jnp.float32)] * 4
        + [jax.ShapeDtypeStruct((PEER_HEADS, s), jnp.float32)],
        compiler_params=_params(_ARB1),
        name="router",
    )(q, keys_1, keys_2)


PEER_TM = 512
PEER_EC = 512
_A_PER_STEP = PEER_EC // PEER_KEYS


def _gelu(x):
    return 0.5 * x * (1.0 + lax.erf(x * (2.0 ** -0.5)))


_PEER_CHUNKS = PEER_EXPERTS // PEER_EC
_MXU_N = 256
_ACT_ROWS = 32


def _peer_kernel(h_ref, u_ref, v_ref, s1_ref, e1_ref, s2_ref, e2_ref, thr_ref, o_ref, sc0, sc1, act0, act1):
    step = pl.program_id(0)
    sc_ref, act_ref = (sc0, sc1), (act0, act1)

    @pl.when(step == 0)
    def _():
        for ref in sc_ref + act_ref:
            ref[...] = jnp.zeros(ref.shape, ref.dtype)

    @pl.when(jnp.maximum(step - 2, 0) % _PEER_CHUNKS == 0)
    def _():
        o_ref[...] = jnp.zeros(o_ref.shape, jnp.float32)

    tm, d_model = h_ref.shape

    def stages(cur, nxt):
        def value_sum(nt):
            dcol = slice(nt * _MXU_N, (nt + 1) * _MXU_N)
            o_ref[:, dcol] += _dot_tn(act_ref[cur][...], v_ref[:, dcol])

        def activations(al, lt, bs):
            tok = slice(lt * LANES, (lt + 1) * LANES)
            keys2 = slice(bs * _ACT_ROWS, (bs + 1) * _ACT_ROWS)
            gate = jnp.zeros((_ACT_ROWS, LANES), jnp.float32)
            for h in range(PEER_HEADS):
                cand = s1_ref[al, h:h + 1, tok] + s2_ref[h, keys2, tok]
                w = e1_ref[al, h:h + 1, tok] * e2_ref[h, keys2, tok]
                gate = gate + jnp.where(cand >= thr_ref[h:h + 1, tok], w, 0.0)
            ex = slice(al * PEER_KEYS + bs * _ACT_ROWS, al * PEER_KEYS + (bs + 1) * _ACT_ROWS)
            act_ref[nxt][ex, tok] = (gate * _gelu(sc_ref[cur][ex, tok])).astype(act_ref[nxt].dtype)

        def scores(nt):
            tok = slice(nt * _MXU_N, (nt + 1) * _MXU_N)
            sc_ref[nxt][:, tok] = _dot_nt(u_ref[...], h_ref[tok, :])

        s3 = [functools.partial(value_sum, nt) for nt in range(d_model // _MXU_N)]
        s2 = [functools.partial(activations, al, lt, bs) for al in range(_A_PER_STEP)
              for lt in range(tm // LANES) for bs in range(PEER_KEYS // _ACT_ROWS)]
        s1 = [functools.partial(scores, nt) for nt in range(tm // _MXU_N)]
        rounds = max(len(s3), len(s2))
        for k in range(rounds):
            for pieces in (s3, s2, s1):
                lo, hi = k * len(pieces) // rounds, (k + 1) * len(pieces) // rounds
                for piece in pieces[lo:hi]:
                    piece()

    @pl.when(step % 2 == 0)
    def _():
        stages(0, 1)

    @pl.when(step % 2 == 1)
    def _():
        stages(1, 0)


def _peer(h2, u16, v16, s1, e1, s2, e2, thr):
    s, d = h2.shape
    tm, ec = PEER_TM, PEER_EC
    npairs = (s // tm) * _PEER_CHUNKS

    def pair(step, lag):
        idx = jnp.clip(step - lag, 0, npairs - 1)
        return idx // _PEER_CHUNKS, idx % _PEER_CHUNKS

    def tile(lag):
        return lambda t: (pair(t, lag)[0], 0)

    def chunk(lag):
        return lambda t: (pair(t, lag)[1], 0)

    a_major = pl.BlockSpec((_A_PER_STEP, PEER_HEADS, tm), lambda t: (pair(t, 1)[1], 0, pair(t, 1)[0]))
    h_major = pl.BlockSpec((PEER_HEADS, PEER_KEYS, tm), lambda t: (0, 0, pair(t, 1)[0]))
    return pl.pallas_call(
        _peer_kernel,
        grid=(npairs + 2,),
        in_specs=[pl.BlockSpec((tm, d), tile(0)),
                  pl.BlockSpec((ec, d), chunk(0)),
                  pl.BlockSpec((ec, d), chunk(2)),
                  a_major, a_major, h_major, h_major,
                  pl.BlockSpec((PEER_HEADS, tm), lambda t: (0, pair(t, 1)[0]))],
        out_specs=pl.BlockSpec((tm, d), tile(2)),
        out_shape=jax.ShapeDtypeStruct((s, d), jnp.float32),
        scratch_shapes=[pltpu.VMEM((ec, tm), jnp.float32)] * 2 + [pltpu.VMEM((ec, tm), jnp.bfloat16)] * 2,
        compiler_params=_params(_ARB1),
        name="peer",
    )(h2, u16, v16, s1, e1, s2, e2, thr)


def _rope_tables(positions):
    half = HEAD_DIM // 2
    inv_freq = ROPE_THETA ** (-jnp.arange(half, dtype=jnp.float32) / half)
    ang = positions.astype(jnp.float32)[:, None] * inv_freq[None, :]
    cos, sin = jnp.cos(ang), jnp.sin(ang)
    return jnp.concatenate([cos, cos], axis=-1), jnp.concatenate([-sin, sin], axis=-1)


def _layer(x, c_col, positions, w_ada, b_ada, attn_pre_norm, attn_post_norm, w_in, conv_w, a_log, dt_bias,
           dn_norm_w, w_o_attn, w_o_dn, w_out, ffn_pre_norm, ffn_post_norm, w_peer_q, peer_keys_1,
           peer_keys_2, expert_u, expert_v):
    d = D_MODEL
    bf = jnp.bfloat16
    mod = _adaln(c_col, w_ada, b_ada[None, :])
    shift_mix, scale_mix, gate_mix, shift_ffn, scale_ffn, gate_ffn = [mod[:, i * d:(i + 1) * d] for i in range(6)]

    h = _prenorm(x, attn_pre_norm[None, :], shift_mix, scale_mix)
    w_in16 = w_in.astype(bf)
    p_attn = _matmul_cols(h, w_in16, W_ATTN, W_DN - W_ATTN, jnp.float32, 1024, 512, "in_proj_attn")
    p_dn = _matmul_cols(h, w_in16, W_DN, W_SMALL - W_DN, bf, 1024, 512, "in_proj_dn")
    p_gates = _matmul(h, w_in16[:, W_GATES:], bf, 1024, 512, "in_proj_gates")
    w_small = jnp.pad(w_in16[:, W_SMALL:W_GATES], ((0, 0), (0, LANES - 2 * DN_HEADS)))
    small_t = _small_proj(h, w_small)

    cos, sin = _rope_tables(positions)
    outs, lses = [], []
    for g, (_, dilation) in enumerate(DILATED_GROUPS):
        o, lse = _attention_group(p_attn, cos, sin, g, dilation)
        outs.append(o)
        lses.append(lse)

    qkvn = _dn_prep(p_dn, conv_w)
    y_d = _delta(qkvn, p_dn, small_t, a_log, dt_bias, dn_norm_w[None, :])

    merged = _merge(outs, lses, y_d, w_o_attn.astype(bf), w_o_dn.astype(bf), p_gates)
    mix = _matmul(merged, w_out.astype(bf), jnp.float32, 1024, 512, "out_proj")
    x1, h2 = _post_mix(mix, x, gate_mix, attn_post_norm[None, :], ffn_pre_norm[None, :], shift_ffn, scale_ffn)

    q = _matmul(h2, w_peer_q.astype(bf), jnp.float32, 1024, 512, "peer_q")
    s1, e1, s2, e2, thr = _router(q, peer_keys_1, peer_keys_2)
    s1, e1 = jnp.transpose(s1, (1, 0, 2)), jnp.transpose(e1, (1, 0, 2))
    y = _peer(h2, expert_u.astype(bf), expert_v.astype(bf), s1, e1, s2, e2, thr)
    return _final(y, x1, gate_ffn, ffn_post_norm[None, :])


def kernel(x, c, positions, w_ada, b_ada, attn_pre_norm, attn_post_norm, w_in, conv_w, a_log, dt_bias, dn_norm_w,
           w_o_attn, w_o_dn, w_out, ffn_pre_norm, ffn_post_norm, w_peer_q, peer_keys_1, peer_keys_2, expert_u,
           expert_v):
    batch, seq, d = x.shape
    depth = w_ada.shape[0]
    outs = []
    for b in range(batch):
        xb = x[b]
        c_col = c[b][:, None]
        for l in range(depth):
            xb = _layer(xb, c_col, positions[b], w_ada[l], b_ada[l], attn_pre_norm[l], attn_post_norm[l], w_in[l],
                        conv_w[l], a_log[l], dt_bias[l], dn_norm_w[l], w_o_attn[l], w_o_dn[l], w_out[l],
                        ffn_pre_norm[l], ffn_post_norm[l], w_peer_q[l], peer_keys_1[l], peer_keys_2[l],
                        expert_u[l], expert_v[l])
        outs.append(xb)
    return jnp.stack(outs, axis=0)
```

```python
import functools

import jax
import jax.numpy as jnp
from jax import lax
from jax.experimental import pallas as pl
from jax.experimental.pallas import tpu as pltpu

D_MODEL = 4096
NORM_EPS = 1e-6
NEG_INF = -1e30

HEAD_DIM = 128
ROPE_THETA = 10000.0
DILATED_GROUPS = ((128, 1), (512, 4), (2048, 16))
HEADS_PER_GROUP = 4
A_WIDTH = 1536
A_OUT_WIDTH = HEADS_PER_GROUP * HEAD_DIM
ATTN_BLOCK = 128
ATTN_SPAN = 2048

DN_HEADS = 16
DN_DK = 128
DN_CONV = 4
DN_CHUNK = 64
DN_K_WIDTH = DN_HEADS * DN_DK
DN_QKV_WIDTH = 3 * DN_K_WIDTH

PEER_HEADS = 8
PEER_KEYS = 128
PEER_EXPERTS = PEER_KEYS * PEER_KEYS
PEER_HALF = 128
PEER_TOPK = 16

P_ATTN_Q = 0
P_ATTN_K = A_WIDTH
P_ATTN_V = 2 * A_WIDTH
P_DN_QKV = 0
P_DN_Z = DN_QKV_WIDTH
P_GATE_A = 0
P_GATE_D = D_MODEL
W_ATTN = 0
W_DN = 3 * A_WIDTH
W_Z = W_DN + DN_QKV_WIDTH
W_SMALL = W_Z + DN_K_WIDTH
W_GATES = W_SMALL + 2 * DN_HEADS

LANES = 128
VMEM_LIMIT = 52 * 1024 * 1024

_ARB1 = ("arbitrary",)
_ARB2 = ("arbitrary", "arbitrary")


def _params(sem):
    return pltpu.CompilerParams(dimension_semantics=sem, vmem_limit_bytes=VMEM_LIMIT)


def _sigmoid(x):
    return 1.0 / (1.0 + jnp.exp(-x))


def _dot(a, b):
    return jnp.dot(a, b, preferred_element_type=jnp.float32)


def _dot_nt(a, b):
    return lax.dot_general(a, b, (((1,), (1,)), ((), ())), preferred_element_type=jnp.float32)


def _dot_tn(a, b):
    return lax.dot_general(a, b, (((0,), (0,)), ((), ())), preferred_element_type=jnp.float32)


def _rms(x, w):
    return x * lax.rsqrt(jnp.mean(x * x, axis=-1, keepdims=True) + NORM_EPS) * w


def _adaln_kernel(c_ref, w_ref, b_ref, o_ref):
    c = c_ref[...]
    act = c * _sigmoid(c)
    o_ref[...] = jnp.sum(act * w_ref[...], axis=0, keepdims=True) + b_ref[...]


def _adaln(c_col, w, b, tn=512):
    d, n = w.shape
    return pl.pallas_call(
        _adaln_kernel,
        grid=(n // tn,),
        in_specs=[pl.BlockSpec((d, 1), lambda j: (0, 0)),
                  pl.BlockSpec((d, tn), lambda j: (0, j)),
                  pl.BlockSpec((1, tn), lambda j: (0, j))],
        out_specs=pl.BlockSpec((1, tn), lambda j: (0, j)),
        out_shape=jax.ShapeDtypeStruct((1, n), jnp.float32),
        compiler_params=_params(_ARB1),
        name="adaln",
    )(c_col, w, b)


def _prenorm_kernel(x_ref, nw_ref, shift_ref, scale_ref, o_ref):
    y = _rms(x_ref[...], nw_ref[...])
    o_ref[...] = (y * (1.0 + scale_ref[...]) + shift_ref[...]).astype(o_ref.dtype)


def _prenorm(x, nw, shift, scale, tm=256):
    s, d = x.shape
    row = pl.BlockSpec((1, d), lambda i: (0, 0))
    return pl.pallas_call(
        _prenorm_kernel,
        grid=(s // tm,),
        in_specs=[pl.BlockSpec((tm, d), lambda i: (i, 0)), row, row, row],
        out_specs=pl.BlockSpec((tm, d), lambda i: (i, 0)),
        out_shape=jax.ShapeDtypeStruct((s, d), jnp.bfloat16),
        compiler_params=_params(_ARB1),
        name="prenorm",
    )(x, nw, shift, scale)


def _mm_kernel(a_ref, b_ref, o_ref):
    o_ref[...] = _dot(a_ref[...], b_ref[...]).astype(o_ref.dtype)


def _matmul(a, b, out_dtype, tm, tn, name):
    m, k = a.shape
    n = b.shape[1]
    return pl.pallas_call(
        _mm_kernel,
        grid=(m // tm, n // tn),
        in_specs=[pl.BlockSpec((tm, k), lambda i, j: (i, 0)),
                  pl.BlockSpec((k, tn), lambda i, j: (0, j))],
        out_specs=pl.BlockSpec((tm, tn), lambda i, j: (i, j)),
        out_shape=jax.ShapeDtypeStruct((m, n), out_dtype),
        compiler_params=_params(_ARB2),
        name=name,
    )(a, b)


def _matmul_cols(a, b, col0, n, out_dtype, tm, tn, name):
    m, k = a.shape
    cb0 = col0 // tn
    return pl.pallas_call(
        _mm_kernel,
        grid=(m // tm, n // tn),
        in_specs=[pl.BlockSpec((tm, k), lambda i, j: (i, 0)),
                  pl.BlockSpec((k, tn), lambda i, j: (0, cb0 + j))],
        out_specs=pl.BlockSpec((tm, tn), lambda i, j: (i, j)),
        out_shape=jax.ShapeDtypeStruct((m, n), out_dtype),
        compiler_params=_params(_ARB2),
        name=name,
    )(a, b)


def _small_kernel(h_ref, w_ref, ot_ref):
    ot_ref[...] = _dot(h_ref[...], w_ref[...]).T[:ot_ref.shape[0], :]


def _small_proj(h, w_pad, tm=512):
    s, d = h.shape
    r = 2 * DN_HEADS
    return pl.pallas_call(
        _small_kernel,
        grid=(s // tm,),
        in_specs=[pl.BlockSpec((tm, d), lambda i: (i, 0)),
                  pl.BlockSpec((d, LANES), lambda i: (0, 0))],
        out_specs=pl.BlockSpec((r, tm), lambda i: (0, i)),
        out_shape=jax.ShapeDtypeStruct((r, s), jnp.float32),
        compiler_params=_params(_ARB1),
        name="small_proj",
    )(h, w_pad)


def _rope(x, cos, sin):
    return x * cos + pltpu.roll(x, HEAD_DIM // 2, 1) * sin


def _attn_kernel(qc_ref, kc_ref, vc_ref, kp_ref, vp_ref, cc_ref, sc_ref, cp_ref, sp_ref, o_ref, l_ref, *, dilation):
    n = pl.program_id(0)
    sub = ATTN_BLOCK * dilation
    qi = lax.broadcasted_iota(jnp.int32, (ATTN_BLOCK, ATTN_BLOCK), 0)
    kj = lax.broadcasted_iota(jnp.int32, (ATTN_BLOCK, ATTN_BLOCK), 1)
    mask_c = kj <= qi
    mask_in = kj >= qi
    mask_first = jnp.logical_and(mask_in, n > 0)
    scale = HEAD_DIM ** -0.5

    def rows_of(sb, r):
        return pl.ds(sb * sub + r, ATTN_BLOCK, stride=dilation) if dilation > 1 else pl.ds(sb * sub, ATTN_BLOCK)

    def scores(sb, r):
        rows = rows_of(sb, r)
        cos_c, sin_c = cc_ref[rows, :], sc_ref[rows, :]
        if sb == 0:
            prow = rows_of(0, r)
            k_prev, v_prev, cos_p, sin_p, mask_p = kp_ref[prow, :], vp_ref[prow, :], cp_ref[prow, :], sp_ref[prow, :], mask_first
        else:
            prow = rows_of(sb - 1, r)
            k_prev, v_prev, cos_p, sin_p, mask_p = kc_ref[prow, :], vc_ref[prow, :], cc_ref[prow, :], sc_ref[prow, :], mask_in
        q = _rope(qc_ref[rows, :], cos_c, sin_c).astype(jnp.bfloat16)
        k_c = _rope(kc_ref[rows, :], cos_c, sin_c).astype(jnp.bfloat16)
        k_p = _rope(k_prev, cos_p, sin_p).astype(jnp.bfloat16)
        s_c = jnp.where(mask_c, _dot_nt(q, k_c) * scale, NEG_INF)
        s_p = jnp.where(mask_p, _dot_nt(q, k_p) * scale, NEG_INF)
        return rows, s_c, s_p, v_prev

    def finish(rows, s_c, s_p, v_prev):
        m = jnp.maximum(jnp.max(s_c, axis=-1, keepdims=True), jnp.max(s_p, axis=-1, keepdims=True))
        p_c = jnp.exp(s_c - m)
        p_p = jnp.exp(s_p - m)
        den = jnp.sum(p_c, axis=-1, keepdims=True) + jnp.sum(p_p, axis=-1, keepdims=True)
        pv = (_dot(p_c.astype(jnp.bfloat16), vc_ref[rows, :].astype(jnp.bfloat16))
              + _dot(p_p.astype(jnp.bfloat16), v_prev.astype(jnp.bfloat16)))
        o_ref[rows, :] = pv / den
        l_ref[rows, :] = jnp.broadcast_to(m + jnp.log(den), (ATTN_BLOCK, HEAD_DIM))

    items = [(sb, r) for sb in range(ATTN_SPAN // sub) for r in range(dilation)]
    batch = 4
    for b0 in range(0, len(items), batch):
        staged = [scores(sb, r) for sb, r in items[b0:b0 + batch]]
        for st in staged:
            finish(*st)


def _attention_group(p, cos, sin, g, dilation):
    s = p.shape[0]
    span = ATTN_SPAN
    sub = ATTN_BLOCK * dilation
    per = span // sub
    qb, kb, vb = [(off + g * A_OUT_WIDTH) // HEAD_DIM for off in (P_ATTN_Q, P_ATTN_K, P_ATTN_V)]

    def cur(off):
        return pl.BlockSpec((span, HEAD_DIM), lambda n, h: (n, off + h))

    def prev(off):
        return pl.BlockSpec((sub, HEAD_DIM), lambda n, h: (jnp.maximum(n * per - 1, 0), off + h))

    tab_c = pl.BlockSpec((span, HEAD_DIM), lambda n, h: (n, 0))
    tab_p = pl.BlockSpec((sub, HEAD_DIM), lambda n, h: (jnp.maximum(n * per - 1, 0), 0))
    out = pl.BlockSpec((span, HEAD_DIM), lambda n, h: (n, h))
    return pl.pallas_call(
        functools.partial(_attn_kernel, dilation=dilation),
        grid=(s // span, HEADS_PER_GROUP),
        in_specs=[cur(qb), cur(kb), cur(vb), prev(kb), prev(vb), tab_c, tab_c, tab_p, tab_p],
        out_specs=[out, out],
        out_shape=[jax.ShapeDtypeStruct((s, A_OUT_WIDTH), jnp.float32)] * 2,
        compiler_params=_params(_ARB2),
        name=f"attn_d{dilation}",
    )(p, p, p, p, p, cos, sin, cos, sin)


DN_PREP_COLS = 512
DN_HALO = 16
_QK_BLOCKS = 2 * DN_K_WIDTH // DN_PREP_COLS


def _dnprep_kernel(x_ref, halo_ref, w_ref, o_ref):
    i = pl.program_id(0)
    j = pl.program_id(1)
    x = x_ref[...].astype(jnp.float32)
    tm = x.shape[0]
    halo = jnp.where(i > 0, halo_ref[...].astype(jnp.float32), 0.0)
    xx = jnp.concatenate([halo, x], axis=0)
    w = w_ref[...]
    acc = x * w[DN_CONV - 1:DN_CONV, :]
    for sft in range(1, DN_CONV):
        acc = acc + xx[DN_HALO - sft:DN_HALO - sft + tm, :] * w[DN_CONV - 1 - sft:DN_CONV - sft, :]
    y = acc * _sigmoid(acc)
    for hh in range(DN_PREP_COLS // DN_DK):
        sl = slice(hh * DN_DK, (hh + 1) * DN_DK)
        yh = y[:, sl]
        f = lax.rsqrt(jnp.sum(yh * yh, axis=-1, keepdims=True) + NORM_EPS)
        f = jnp.where(j < _QK_BLOCKS, f, 1.0)
        o_ref[:, sl] = (yh * f).astype(o_ref.dtype)


def _dn_prep(p, conv_w, tm=512):
    s = p.shape[0]
    cb0 = P_DN_QKV // DN_PREP_COLS
    return pl.pallas_call(
        _dnprep_kernel,
        grid=(s // tm, DN_QKV_WIDTH // DN_PREP_COLS),
        in_specs=[pl.BlockSpec((tm, DN_PREP_COLS), lambda i, j: (i, cb0 + j)),
                  pl.BlockSpec((DN_HALO, DN_PREP_COLS),
                               lambda i, j: (jnp.maximum(i * (tm // DN_HALO) - 1, 0), cb0 + j)),
                  pl.BlockSpec((DN_CONV, DN_PREP_COLS), lambda i, j: (0, j))],
        out_specs=pl.BlockSpec((tm, DN_PREP_COLS), lambda i, j: (i, j)),
        out_shape=jax.ShapeDtypeStruct((s, DN_QKV_WIDTH), jnp.bfloat16),
        compiler_params=_params(_ARB2),
        name="dn_prep",
    )(p, p, conv_w)


DELTA_CHUNK = 128
DELTA_HG = 16


def _softplus(x):
    return jnp.maximum(x, 0.0) + jnp.log1p(jnp.exp(-jnp.abs(x)))


def _delta_local_kernel(q_ref, k_ref, v_ref, braw_ref, araw_ref, alog_ref, dtb_ref, u_ref, wq_ref, akt_ref, dl_ref):
    c = DELTA_CHUNK
    hi = lax.Precision.HIGHEST
    beta_t = _sigmoid(braw_ref[...])
    g_t = -jnp.exp(alog_ref[...]) * _softplus(araw_ref[...] + dtb_ref[...])

    ii = lax.broadcasted_iota(jnp.int32, (c, c), 0)
    jj = lax.broadcasted_iota(jnp.int32, (c, c), 1)
    lower = (ii >= jj).astype(jnp.float32)
    upper = (ii <= jj).astype(jnp.float32)
    eye = (ii == jj).astype(jnp.float32)
    scale = DN_DK ** -0.5

    gcum_row = jnp.dot(g_t, upper, precision=hi, preferred_element_type=jnp.float32)
    gcum_col = lax.dot_general(lower, g_t, (((1,), (1,)), ((), ())), precision=hi,
                               preferred_element_type=jnp.float32)
    beta_col = lax.dot_general(eye, beta_t, (((1,), (1,)), ((), ())), precision=hi,
                               preferred_element_type=jnp.float32)
    heads = range(DELTA_HG)
    cols = [slice(hh * DN_DK, (hh + 1) * DN_DK) for hh in heads]
    lms = []
    for hh in heads:
        q = q_ref[:, cols[hh]].astype(jnp.float32)
        k = k_ref[:, cols[hh]].astype(jnp.float32)
        gcol = gcum_col[:, hh:hh + 1]
        dec = jnp.exp(jnp.minimum(gcol - gcum_row[hh:hh + 1, :], 0.0))
        both = _dot_nt(jnp.concatenate([q, k], axis=0).astype(jnp.bfloat16), k.astype(jnp.bfloat16))
        g_last = gcol[c - 1:c, :]
        akt_ref[hh, :c, :] = (jnp.where(ii >= jj, both[:c] * dec, 0.0) * scale).astype(akt_ref.dtype)
        akt_ref[hh, c:, :] = (k * jnp.exp(g_last - gcol)).T.astype(akt_ref.dtype)
        wq_ref[hh, c:, :] = (q * (scale * jnp.exp(gcol))).astype(wq_ref.dtype)
        dl_ref[hh] = jnp.broadcast_to(jnp.exp(g_last), (1, DN_DK))
        lms.append(jnp.where(ii > jj, both[c:] * dec * beta_col[:, hh:hh + 1], 0.0))
    diff_bits = jnp.bitwise_xor(ii, jj)
    ts = [eye - jnp.where(diff_bits == 1, lm, 0.0) for lm in lms]
    level = 1
    while (2 << level) <= c:
        quad = jnp.right_shift(diff_bits, level) == 1
        offs = [jnp.where(quad, lm, 0.0).astype(jnp.bfloat16) for lm in lms]
        t16s = [t.astype(jnp.bfloat16) for t in ts]
        mos = [_dot(t16, off).astype(jnp.bfloat16) for t16, off in zip(t16s, offs)]
        ts = [t - _dot(mo, t16) for t, mo, t16 in zip(ts, mos, t16s)]
        level += 1
    for hh in heads:
        k = k_ref[:, cols[hh]].astype(jnp.float32)
        v = v_ref[:, cols[hh]].astype(jnp.float32)
        bcol = beta_col[:, hh:hh + 1]
        rhs = jnp.concatenate([v * bcol, k * (bcol * jnp.exp(gcum_col[:, hh:hh + 1]))], axis=1).astype(jnp.bfloat16)
        uw = _dot(ts[hh].astype(jnp.bfloat16), rhs)
        u_ref[:, cols[hh]] = uw[:, :DN_DK]
        wq_ref[hh, :c, :] = uw[:, DN_DK:].astype(wq_ref.dtype)


def _delta_scan_kernel(u_ref, wq_ref, akt_ref, dl_ref, z_ref, nw_ref, o_ref, state_ref):
    c = DELTA_CHUNK

    @pl.when(pl.program_id(0) == 0)
    def _():
        state_ref[...] = jnp.zeros(state_ref.shape, jnp.float32)

    nw = nw_ref[...]
    group = 4
    for h0 in range(0, DN_HEADS, group):
        hs = range(h0, h0 + group)
        cols = {h: slice(h * DN_DK, (h + 1) * DN_DK) for h in hs}
        wqs = {h: _dot(wq_ref[h], state_ref[h].astype(jnp.bfloat16)) for h in hs}
        rs = {h: _dot(akt_ref[h], (u_ref[:, cols[h]] - wqs[h][:c]).astype(jnp.bfloat16)) for h in hs}
        for h in hs:
            state_ref[h] = state_ref[h] * dl_ref[h] + rs[h][c:]
        for h in hs:
            z = z_ref[:, cols[h]].astype(jnp.float32)
            o_ref[:, cols[h]] = (_rms(wqs[h][c:] + rs[h][:c], nw) * (z * _sigmoid(z))).astype(o_ref.dtype)


def _delta(qkvn, p, small_t, a_log, dt_bias, norm_w):
    s = qkvn.shape[0]
    c = DELTA_CHUNK
    nc = s // c
    w = DELTA_HG * DN_DK
    nhg = DN_HEADS // DELTA_HG
    small_g = small_t.reshape(2 * nhg, DELTA_HG, s)
    a_log_g = a_log.reshape(nhg, DELTA_HG, 1)
    dt_bias_g = dt_bias.reshape(nhg, DELTA_HG, 1)

    def blk(off):
        return pl.BlockSpec((c, w), lambda i, j: (i, off + j))

    def head_rows(off):
        return pl.BlockSpec((None, DELTA_HG, c), lambda i, j: (off + j, 0, i))

    per_head = pl.BlockSpec((None, DELTA_HG, 1), lambda i, j: (j, 0, 0))
    stacked = pl.BlockSpec((None, DELTA_HG, 2 * c, DN_DK), lambda i, j: (i, j, 0, 0))
    u, wq, akt, dl = pl.pallas_call(
        _delta_local_kernel,
        grid=(nc, nhg),
        in_specs=[blk(0), blk(nhg), blk(2 * nhg), head_rows(0), head_rows(nhg), per_head, per_head],
        out_specs=[pl.BlockSpec((c, w), lambda i, j: (i, j)), stacked, stacked,
                   pl.BlockSpec((None, DELTA_HG, 1, DN_DK), lambda i, j: (i, j, 0, 0))],
        out_shape=[jax.ShapeDtypeStruct((s, DN_K_WIDTH), jnp.float32),
                   jax.ShapeDtypeStruct((nc, DN_HEADS, 2 * c, DN_DK), jnp.bfloat16),
                   jax.ShapeDtypeStruct((nc, DN_HEADS, 2 * c, DN_DK), jnp.bfloat16),
                   jax.ShapeDtypeStruct((nc, DN_HEADS, 1, DN_DK), jnp.float32)],
        compiler_params=_params(_ARB2),
        name="delta_local",
    )(qkvn, qkvn, qkvn, small_g, small_g, a_log_g, dt_bias_g)

    full = pl.BlockSpec((None, DN_HEADS, 2 * c, DN_DK), lambda i: (i, 0, 0, 0))
    return pl.pallas_call(
        _delta_scan_kernel,
        grid=(nc,),
        in_specs=[pl.BlockSpec((c, DN_K_WIDTH), lambda i: (i, 0)), full, full,
                  pl.BlockSpec((None, DN_HEADS, 1, DN_DK), lambda i: (i, 0, 0, 0)),
                  pl.BlockSpec((c, DN_K_WIDTH), lambda i: (i, P_DN_Z // DN_K_WIDTH)),
                  pl.BlockSpec((1, DN_DK), lambda i: (0, 0))],
        out_specs=pl.BlockSpec((c, DN_K_WIDTH), lambda i: (i, 0)),
        out_shape=jax.ShapeDtypeStruct((s, DN_K_WIDTH), jnp.bfloat16),
        scratch_shapes=[pltpu.VMEM((DN_HEADS, DN_DK, DN_DK), jnp.float32)],
        compiler_params=_params(_ARB1),
        name="delta_scan",
    )(u, wq, akt, dl, p, norm_w)


def _merge_kernel(o0, o1, o2, l0, l1, l2, yd_ref, woa_ref, wod_ref, ga_ref, gd_ref, out_ref, ya_ref):
    @pl.when(pl.program_id(1) == 0)
    def _():
        la, lb, lc = l0[...], l1[...], l2[...]
        m = jnp.maximum(jnp.maximum(la, lb), lc)
        ea, eb, ec = jnp.exp(la - m), jnp.exp(lb - m), jnp.exp(lc - m)
        ya = (o0[...] * ea + o1[...] * eb + o2[...] * ec) / (ea + eb + ec)
        ya_ref[...] = ya.astype(ya_ref.dtype)

    acc_a = _dot(ya_ref[...], woa_ref[...])
    acc_d = _dot(yd_ref[...], wod_ref[...])
    gate_a = _sigmoid(ga_ref[...].astype(jnp.float32))
    gate_d = _sigmoid(gd_ref[...].astype(jnp.float32))
    out_ref[...] = (gate_a * acc_a + gate_d * acc_d).astype(out_ref.dtype)


def _merge(os_, ls_, y_d, w_oa, w_od, p, tm=512, tn=1024):
    s = y_d.shape[0]
    n = w_oa.shape[1]
    att = pl.BlockSpec((tm, A_OUT_WIDTH), lambda i, j: (i, 0))
    gdb = P_GATE_D // tn
    return pl.pallas_call(
        _merge_kernel,
        grid=(s // tm, n // tn),
        in_specs=[att] * 6 + [
            pl.BlockSpec((tm, DN_K_WIDTH), lambda i, j: (i, 0)),
            pl.BlockSpec((A_OUT_WIDTH, tn), lambda i, j: (0, j)),
            pl.BlockSpec((DN_K_WIDTH, tn), lambda i, j: (0, j)),
            pl.BlockSpec((tm, tn), lambda i, j: (i, j)),
            pl.BlockSpec((tm, tn), lambda i, j: (i, gdb + j))],
        out_specs=pl.BlockSpec((tm, tn), lambda i, j: (i, j)),
        out_shape=jax.ShapeDtypeStruct((s, n), jnp.bfloat16),
        scratch_shapes=[pltpu.VMEM((tm, A_OUT_WIDTH), jnp.bfloat16)],
        compiler_params=_params(_ARB2),
        name="merge",
    )(*os_, *ls_, y_d, w_oa, w_od, p, p)


def _postmix_kernel(mix_ref, x_ref, gate_ref, pn_ref, fn_ref, shift_ref, scale_ref, x1_ref, h_ref):
    x1 = x_ref[...] + gate_ref[...] * _rms(mix_ref[...], pn_ref[...])
    x1_ref[...] = x1
    h_ref[...] = (_rms(x1, fn_ref[...]) * (1.0 + scale_ref[...]) + shift_ref[...]).astype(h_ref.dtype)


def _post_mix(mix, x, gate, post_w, ffn_w, shift, scale, tm=256):
    s, d = x.shape
    row = pl.BlockSpec((1, d), lambda i: (0, 0))
    blk = pl.BlockSpec((tm, d), lambda i: (i, 0))
    return pl.pallas_call(
        _postmix_kernel,
        grid=(s // tm,),
        in_specs=[blk, blk, row, row, row, row, row],
        out_specs=[blk, blk],
        out_shape=[jax.ShapeDtypeStruct((s, d), jnp.float32), jax.ShapeDtypeStruct((s, d), jnp.bfloat16)],
        compiler_params=_params(_ARB1),
        name="post_mix",
    )(mix, x, gate, post_w, ffn_w, shift, scale)


def _final_kernel(y_ref, x_ref, gate_ref, pn_ref, o_ref):
    o_ref[...] = x_ref[...] + gate_ref[...] * _rms(y_ref[...], pn_ref[...])


def _final(y, x1, gate, post_w, tm=256):
    s, d = x1.shape
    row = pl.BlockSpec((1, d), lambda i: (0, 0))
    blk = pl.BlockSpec((tm, d), lambda i: (i, 0))
    return pl.pallas_call(
        _final_kernel,
        grid=(s // tm,),
        in_specs=[blk, blk, row, row],
        out_specs=blk,
        out_shape=jax.ShapeDtypeStruct((s, d), jnp.float32),
        compiler_params=_params(_ARB1),
        name="final",
    )(y, x1, gate, post_w)


ROUTER_TM = 256


def _topk_desc(scores):
    work = scores
    vals = []
    for it in range(PEER_TOPK):
        mx = jnp.max(work, axis=0, keepdims=True)
        vals.append(mx)
        if it + 1 < PEER_TOPK:
            work = jnp.where(work == mx, -jnp.inf, work)
    return jnp.concatenate(vals, axis=0)


def _candidate_sums(v1, v2):
    half = PEER_TOPK // 2
    row = lax.broadcasted_iota(jnp.int32, (half, v1.shape[1]), 0)
    parts = [v1[0:1, :] + v2]
    for i in range(1, half):
        parts.append(jnp.where(row < PEER_TOPK // (i + 1), v1[i:i + 1, :] + v2[:half, :], -jnp.inf))
    parts.append(v1[half:, :] + v2[0:1, :])
    return jnp.concatenate(parts, axis=0)


def _router_kernel(q_ref, k1_ref, k2_ref, s1_ref, e1_ref, s2_ref, e2_ref, thr_ref):
    k1 = k1_ref[...].astype(jnp.bfloat16)
    k2 = k2_ref[...].astype(jnp.bfloat16)

    def head(h, carry):
        c0 = pl.multiple_of(h * 2 * PEER_HALF, 2 * PEER_HALF)
        q1 = q_ref[:, pl.ds(c0, PEER_HALF)].astype(jnp.bfloat16)
        q2 = q_ref[:, pl.ds(c0 + PEER_HALF, PEER_HALF)].astype(jnp.bfloat16)
        s1 = _dot_nt(k1, q1)
        s2 = _dot_nt(k2, q2)
        v1 = _topk_desc(s1)
        v2 = _topk_desc(s2)
        cand = _candidate_sums(v1, v2)
        thr = _topk_desc(cand)[PEER_TOPK - 1:PEER_TOPK, :]
        top = v1[0:1, :] + v2[0:1, :]
        zsum = jnp.sum(jnp.where(cand >= thr, jnp.exp(cand - top), 0.0), axis=0, keepdims=True)
        s1_ref[h] = s1
        e1_ref[h] = jnp.exp(s1 - v1[0:1, :]) / zsum
        s2_ref[h] = s2
        e2_ref[h] = jnp.exp(s2 - v2[0:1, :])
        thr_ref[pl.ds(h, 1), :] = thr
        return carry

    lax.fori_loop(0, PEER_HEADS, head, 0)


def _router(q, keys_1, keys_2):
    s = q.shape[0]
    tm = ROUTER_TM
    h_major = pl.BlockSpec((PEER_HEADS, PEER_KEYS, tm), lambda i: (0, 0, i))
    kspec = pl.BlockSpec((PEER_KEYS, PEER_HALF), lambda i: (0, 0))
    return pl.pallas_call(
        _router_kernel,
        grid=(s // tm,),
        in_specs=[pl.BlockSpec((tm, 2 * PEER_HALF * PEER_HEADS), lambda i: (i, 0)), kspec, kspec],
        out_specs=[h_major] * 4 + [pl.BlockSpec((PEER_HEADS, tm), lambda i: (0, i))],
        out_shape=[jax.ShapeDtypeStruct((PEER_HEADS, PEER_KEYS, s), jnp.float32)] * 4
        + [jax.ShapeDtypeStruct((PEER_HEADS, s), jnp.float32)],
        compiler_params=_params(_ARB1),
        name="router",
    )(q, keys_1, keys_2)


PEER_TM = 512
PEER_EC = 512
_A_PER_STEP = PEER_EC // PEER_KEYS


def _gelu(x):
    return 0.5 * x * (1.0 + lax.erf(x * (2.0 ** -0.5)))


_PEER_CHUNKS = PEER_EXPERTS // PEER_EC
_MXU_N = 256
_ACT_ROWS = 32
_SCORE_K = 512


def _peer_kernel(h_ref, u_ref, v_ref, s1_ref, e1_ref, s2_ref, e2_ref, thr_ref, o_ref, sc0, sc1, act0, act1):
    step = pl.program_id(0)
    sc_ref, act_ref = (sc0, sc1), (act0, act1)

    @pl.when(step == 0)
    def _():
        sc0[...] = jnp.zeros(sc0.shape, sc0.dtype)
        act0[...] = jnp.zeros(act0.shape, act0.dtype)

    @pl.when(step > 0)
    def _():
        sc0[...] = sc1[...]
        act0[...] = act1[...]

    @pl.when(jnp.maximum(step - 2, 0) % _PEER_CHUNKS == 0)
    def _():
        o_ref[...] = jnp.zeros(o_ref.shape, jnp.float32)

    tm, d_model = h_ref.shape

    def stages(cur, nxt):
        def value_sum(nt):
            dcol = slice(nt * _MXU_N, (nt + 1) * _MXU_N)
            o_ref[:, dcol] += _dot_tn(act_ref[cur][...], v_ref[:, dcol])

        def activations(al, lt, bs):
            tok = slice(lt * LANES, (lt + 1) * LANES)
            keys2 = slice(bs * _ACT_ROWS, (bs + 1) * _ACT_ROWS)
            gate = jnp.zeros((_ACT_ROWS, LANES), jnp.float32)
            for h in range(PEER_HEADS):
                cand = s1_ref[al, h:h + 1, tok] + s2_ref[h, keys2, tok]
                w = e1_ref[al, h:h + 1, tok] * e2_ref[h, keys2, tok]
                gate = gate + jnp.where(cand >= thr_ref[h:h + 1, tok], w, 0.0)
            ex = slice(al * PEER_KEYS + bs * _ACT_ROWS, al * PEER_KEYS + (bs + 1) * _ACT_ROWS)
            act_ref[nxt][ex, tok] = (gate * _gelu(sc_ref[cur][ex, tok])).astype(act_ref[nxt].dtype)

        def scores(nt, kt):
            tok = slice(nt * _MXU_N, (nt + 1) * _MXU_N)
            ks = slice(kt * _SCORE_K, (kt + 1) * _SCORE_K)
            part = _dot_nt(u_ref[:, ks], h_ref[tok, ks])
            if kt == 0:
                sc_ref[nxt][:, tok] = part
            else:
                sc_ref[nxt][:, tok] += part

        s3 = [functools.partial(value_sum, nt) for nt in range(d_model // _MXU_N)]
        s2 = [functools.partial(activations, al, lt, bs) for al in range(_A_PER_STEP)
              for lt in range(tm // LANES) for bs in range(PEER_KEYS // _ACT_ROWS)]
        s1 = [functools.partial(scores, nt, kt) for nt in range(tm // _MXU_N) for kt in range(d_model // _SCORE_K)]
        rounds = max(len(s3), len(s2))
        for k in range(rounds):
            for pieces in (s3, s2, s1):
                lo, hi = k * len(pieces) // rounds, (k + 1) * len(pieces) // rounds
                for piece in pieces[lo:hi]:
                    piece()

    stages(0, 1)


def _peer(h2, u16, v16, s1, e1, s2, e2, thr):
    s, d = h2.shape
    tm, ec = PEER_TM, PEER_EC
    npairs = (s // tm) * _PEER_CHUNKS

    def pair(step, lag):
        idx = jnp.clip(step - lag, 0, npairs - 1)
        return idx // _PEER_CHUNKS, idx % _PEER_CHUNKS

    def tile(lag):
        return lambda t: (pair(t, lag)[0], 0)

    def chunk(lag):
        return lambda t: (pair(t, lag)[1], 0)

    a_major = pl.BlockSpec((_A_PER_STEP, PEER_HEADS, tm), lambda t: (pair(t, 1)[1], 0, pair(t, 1)[0]))
    h_major = pl.BlockSpec((PEER_HEADS, PEER_KEYS, tm), lambda t: (0, 0, pair(t, 1)[0]))
    return pl.pallas_call(
        _peer_kernel,
        grid=(npairs + 2,),
        in_specs=[pl.BlockSpec((tm, d), tile(0)),
                  pl.BlockSpec((ec, d), chunk(0)),
                  pl.BlockSpec((ec, d), chunk(2)),
                  a_major, a_major, h_major, h_major,
                  pl.BlockSpec((PEER_HEADS, tm), lambda t: (0, pair(t, 1)[0]))],
        out_specs=pl.BlockSpec((tm, d), tile(2)),
        out_shape=jax.ShapeDtypeStruct((s, d), jnp.float32),
        scratch_shapes=[pltpu.VMEM((ec, tm), jnp.float32)] * 2 + [pltpu.VMEM((ec, tm), jnp.bfloat16)] * 2,
        compiler_params=_params(_ARB1),
        name="peer",
    )(h2, u16, v16, s1, e1, s2, e2, thr)


def _rope_tables(positions):
    half = HEAD_DIM // 2
    inv_freq = ROPE_THETA ** (-jnp.arange(half, dtype=jnp.float32) / half)
    ang = positions.astype(jnp.float32)[:, None] * inv_freq[None, :]
    cos, sin = jnp.cos(ang), jnp.sin(ang)
    return jnp.concatenate([cos, cos], axis=-1), jnp.concatenate([-sin, sin], axis=-1)


def _layer(x, c_col, positions, w_ada, b_ada, attn_pre_norm, attn_post_norm, w_in, conv_w, a_log, dt_bias,
           dn_norm_w, w_o_attn, w_o_dn, w_out, ffn_pre_norm, ffn_post_norm, w_peer_q, peer_keys_1,
           peer_keys_2, expert_u, expert_v):
    d = D_MODEL
    bf = jnp.bfloat16
    mod = _adaln(c_col, w_ada, b_ada[None, :])
    shift_mix, scale_mix, gate_mix, shift_ffn, scale_ffn, gate_ffn = [mod[:, i * d:(i + 1) * d] for i in range(6)]

    h = _prenorm(x, attn_pre_norm[None, :], shift_mix, scale_mix)
    w_in16 = w_in.astype(bf)
    p_attn = _matmul_cols(h, w_in16, W_ATTN, W_DN - W_ATTN, jnp.float32, 1024, 512, "in_proj_attn")
    p_dn = _matmul_cols(h, w_in16, W_DN, W_SMALL - W_DN, bf, 1024, 512, "in_proj_dn")
    p_gates = _matmul(h, w_in16[:, W_GATES:], bf, 1024, 512, "in_proj_gates")
    w_small = jnp.pad(w_in16[:, W_SMALL:W_GATES], ((0, 0), (0, LANES - 2 * DN_HEADS)))
    small_t = _small_proj(h, w_small)

    cos, sin = _rope_tables(positions)
    outs, lses = [], []
    for g, (_, dilation) in enumerate(DILATED_GROUPS):
        o, lse = _attention_group(p_attn, cos, sin, g, dilation)
        outs.append(o)
        lses.append(lse)

    qkvn = _dn_prep(p_dn, conv_w)
    y_d = _delta(qkvn, p_dn, small_t, a_log, dt_bias, dn_norm_w[None, :])

    merged = _merge(outs, lses, y_d, w_o_attn.astype(bf), w_o_dn.astype(bf), p_gates)
    mix = _matmul(merged, w_out.astype(bf), jnp.float32, 1024, 512, "out_proj")
    x1, h2 = _post_mix(mix, x, gate_mix, attn_post_norm[None, :], ffn_pre_norm[None, :], shift_ffn, scale_ffn)

    q = _matmul(h2, w_peer_q.astype(bf), jnp.float32, 1024, 512, "peer_q")
    s1, e1, s2, e2, thr = _router(q, peer_keys_1, peer_keys_2)
    s1, e1 = jnp.transpose(s1, (1, 0, 2)), jnp.transpose(e1, (1, 0, 2))
    y = _peer(h2, expert_u.astype(bf), expert_v.astype(bf), s1, e1, s2, e2, thr)
    return _final(y, x1, gate_ffn, ffn_post_norm[None, :])


def kernel(x, c, positions, w_ada, b_ada, attn_pre_norm, attn_post_norm, w_in, conv_w, a_log, dt_bias, dn_norm_w,
           w_o_attn, w_o_dn, w_out, ffn_pre_norm, ffn_post_norm, w_peer_q, peer_keys_1, peer_keys_2, expert_u,
           expert_v):
    batch, seq, d = x.shape
    depth = w_ada.shape[0]
    outs = []
    for b in range(batch):
        xb = x[b]
        c_col = c[b][:, None]
        for l in range(depth):
            xb = _layer(xb, c_col, positions[b], w_ada[l], b_ada[l], attn_pre_norm[l], attn_post_norm[l], w_in[l],
                        conv_w[l], a_log[l], dt_bias[l], dn_norm_w[l], w_o_attn[l], w_o_dn[l], w_out[l],
                        ffn_pre_norm[l], ffn_post_norm[l], w_peer_q[l], peer_keys_1[l], peer_keys_2[l],
                        expert_u[l], expert_v[l])
        outs.append(xb)
    return jnp.stack(outs, axis=0)
```

```python
import functools

import jax
import jax.numpy as jnp
from jax import lax
from jax.experimental import pallas as pl
from jax.experimental.pallas import tpu as pltpu

D_MODEL = 4096
NORM_EPS = 1e-6
NEG_INF = -1e30

HEAD_DIM = 128
ROPE_THETA = 10000.0
DILATED_GROUPS = ((128, 1), (512, 4), (2048, 16))
HEADS_PER_GROUP = 4
A_WIDTH = 1536
A_OUT_WIDTH = HEADS_PER_GROUP * HEAD_DIM
ATTN_BLOCK = 128
ATTN_SPAN = 2048

DN_HEADS = 16
DN_DK = 128
DN_CONV = 4
DN_CHUNK = 64
DN_K_WIDTH = DN_HEADS * DN_DK
DN_QKV_WIDTH = 3 * DN_K_WIDTH

PEER_HEADS = 8
PEER_KEYS = 128
PEER_EXPERTS = PEER_KEYS * PEER_KEYS
PEER_HALF = 128
PEER_TOPK = 16

P_ATTN_Q = 0
P_ATTN_K = A_WIDTH
P_ATTN_V = 2 * A_WIDTH
P_DN_QKV = 0
P_DN_Z = DN_QKV_WIDTH
P_GATE_A = 0
P_GATE_D = D_MODEL
W_ATTN = 0
W_DN = 3 * A_WIDTH
W_Z = W_DN + DN_QKV_WIDTH
W_SMALL = W_Z + DN_K_WIDTH
W_GATES = W_SMALL + 2 * DN_HEADS

LANES = 128
VMEM_LIMIT = 52 * 1024 * 1024

_ARB1 = ("arbitrary",)
_ARB2 = ("arbitrary", "arbitrary")


def _params(sem):
    return pltpu.CompilerParams(dimension_semantics=sem, vmem_limit_bytes=VMEM_LIMIT)


def _sigmoid(x):
    return 1.0 / (1.0 + jnp.exp(-x))


def _dot(a, b):
    return jnp.dot(a, b, preferred_element_type=jnp.float32)


def _dot_nt(a, b):
    return lax.dot_general(a, b, (((1,), (1,)), ((), ())), preferred_element_type=jnp.float32)


def _dot_tn(a, b):
    return lax.dot_general(a, b, (((0,), (0,)), ((), ())), preferred_element_type=jnp.float32)


def _rms(x, w):
    return x * lax.rsqrt(jnp.mean(x * x, axis=-1, keepdims=True) + NORM_EPS) * w


def _adaln_kernel(c_ref, w_ref, b_ref, o_ref):
    c = c_ref[...]
    act = c * _sigmoid(c)
    o_ref[...] = jnp.sum(act * w_ref[...], axis=0, keepdims=True) + b_ref[...]


def _adaln(c_col, w, b, tn=512):
    d, n = w.shape
    return pl.pallas_call(
        _adaln_kernel,
        grid=(n // tn,),
        in_specs=[pl.BlockSpec((d, 1), lambda j: (0, 0)),
                  pl.BlockSpec((d, tn), lambda j: (0, j)),
                  pl.BlockSpec((1, tn), lambda j: (0, j))],
        out_specs=pl.BlockSpec((1, tn), lambda j: (0, j)),
        out_shape=jax.ShapeDtypeStruct((1, n), jnp.float32),
        compiler_params=_params(_ARB1),
        name="adaln",
    )(c_col, w, b)


def _prenorm_kernel(x_ref, nw_ref, shift_ref, scale_ref, o_ref):
    y = _rms(x_ref[...], nw_ref[...])
    o_ref[...] = (y * (1.0 + scale_ref[...]) + shift_ref[...]).astype(o_ref.dtype)


def _prenorm(x, nw, shift, scale, tm=256):
    s, d = x.shape
    row = pl.BlockSpec((1, d), lambda i: (0, 0))
    return pl.pallas_call(
        _prenorm_kernel,
        grid=(s // tm,),
        in_specs=[pl.BlockSpec((tm, d), lambda i: (i, 0)), row, row, row],
        out_specs=pl.BlockSpec((tm, d), lambda i: (i, 0)),
        out_shape=jax.ShapeDtypeStruct((s, d), jnp.bfloat16),
        compiler_params=_params(_ARB1),
        name="prenorm",
    )(x, nw, shift, scale)


def _mm_kernel(a_ref, b_ref, o_ref):
    o_ref[...] = _dot(a_ref[...], b_ref[...]).astype(o_ref.dtype)


def _matmul(a, b, out_dtype, tm, tn, name):
    m, k = a.shape
    n = b.shape[1]
    return pl.pallas_call(
        _mm_kernel,
        grid=(m // tm, n // tn),
        in_specs=[pl.BlockSpec((tm, k), lambda i, j: (i, 0)),
                  pl.BlockSpec((k, tn), lambda i, j: (0, j))],
        out_specs=pl.BlockSpec((tm, tn), lambda i, j: (i, j)),
        out_shape=jax.ShapeDtypeStruct((m, n), out_dtype),
        compiler_params=_params(_ARB2),
        name=name,
    )(a, b)


def _matmul_cols(a, b, col0, n, out_dtype, tm, tn, name):
    m, k = a.shape
    cb0 = col0 // tn
    return pl.pallas_call(
        _mm_kernel,
        grid=(m // tm, n // tn),
        in_specs=[pl.BlockSpec((tm, k), lambda i, j: (i, 0)),
                  pl.BlockSpec((k, tn), lambda i, j: (0, cb0 + j))],
        out_specs=pl.BlockSpec((tm, tn), lambda i, j: (i, j)),
        out_shape=jax.ShapeDtypeStruct((m, n), out_dtype),
        compiler_params=_params(_ARB2),
        name=name,
    )(a, b)


def _small_kernel(h_ref, w_ref, ot_ref):
    ot_ref[...] = _dot(h_ref[...], w_ref[...]).T[:ot_ref.shape[0], :]


def _small_proj(h, w_pad, tm=512):
    s, d = h.shape
    r = 2 * DN_HEADS
    return pl.pallas_call(
        _small_kernel,
        grid=(s // tm,),
        in_specs=[pl.BlockSpec((tm, d), lambda i: (i, 0)),
                  pl.BlockSpec((d, LANES), lambda i: (0, 0))],
        out_specs=pl.BlockSpec((r, tm), lambda i: (0, i)),
        out_shape=jax.ShapeDtypeStruct((r, s), jnp.float32),
        compiler_params=_params(_ARB1),
        name="small_proj",
    )(h, w_pad)


def _rope(x, cos, sin):
    return x * cos + pltpu.roll(x, HEAD_DIM // 2, 1) * sin


def _attn_kernel(qc_ref, kc_ref, vc_ref, kp_ref, vp_ref, cc_ref, sc_ref, cp_ref, sp_ref, o_ref, l_ref, *, dilation):
    n = pl.program_id(0)
    sub = ATTN_BLOCK * dilation
    qi = lax.broadcasted_iota(jnp.int32, (ATTN_BLOCK, ATTN_BLOCK), 0)
    kj = lax.broadcasted_iota(jnp.int32, (ATTN_BLOCK, ATTN_BLOCK), 1)
    mask_c = kj <= qi
    mask_in = kj >= qi
    mask_first = jnp.logical_and(mask_in, n > 0)
    scale = HEAD_DIM ** -0.5

    def rows_of(sb, r):
        return pl.ds(sb * sub + r, ATTN_BLOCK, stride=dilation) if dilation > 1 else pl.ds(sb * sub, ATTN_BLOCK)

    def scores(sb, r):
        rows = rows_of(sb, r)
        cos_c, sin_c = cc_ref[rows, :], sc_ref[rows, :]
        if sb == 0:
            prow = rows_of(0, r)
            k_prev, v_prev, cos_p, sin_p, mask_p = kp_ref[prow, :], vp_ref[prow, :], cp_ref[prow, :], sp_ref[prow, :], mask_first
        else:
            prow = rows_of(sb - 1, r)
            k_prev, v_prev, cos_p, sin_p, mask_p = kc_ref[prow, :], vc_ref[prow, :], cc_ref[prow, :], sc_ref[prow, :], mask_in
        q = _rope(qc_ref[rows, :], cos_c, sin_c).astype(jnp.bfloat16)
        k_c = _rope(kc_ref[rows, :], cos_c, sin_c).astype(jnp.bfloat16)
        k_p = _rope(k_prev, cos_p, sin_p).astype(jnp.bfloat16)
        s_c = jnp.where(mask_c, _dot_nt(q, k_c) * scale, NEG_INF)
        s_p = jnp.where(mask_p, _dot_nt(q, k_p) * scale, NEG_INF)
        return rows, s_c, s_p, v_prev

    def finish(rows, s_c, s_p, v_prev):
        m = jnp.maximum(jnp.max(s_c, axis=-1, keepdims=True), jnp.max(s_p, axis=-1, keepdims=True))
        p_c = jnp.exp(s_c - m)
        p_p = jnp.exp(s_p - m)
        den = jnp.sum(p_c, axis=-1, keepdims=True) + jnp.sum(p_p, axis=-1, keepdims=True)
        pv = (_dot(p_c.astype(jnp.bfloat16), vc_ref[rows, :].astype(jnp.bfloat16))
              + _dot(p_p.astype(jnp.bfloat16), v_prev.astype(jnp.bfloat16)))
        o_ref[rows, :] = pv / den
        l_ref[rows, :] = jnp.broadcast_to(m + jnp.log(den), (ATTN_BLOCK, HEAD_DIM))

    items = [(sb, r) for sb in range(ATTN_SPAN // sub) for r in range(dilation)]
    batch = 4
    for b0 in range(0, len(items), batch):
        staged = [scores(sb, r) for sb, r in items[b0:b0 + batch]]
        for st in staged:
            finish(*st)


def _attention_group(p, cos, sin, g, dilation):
    s = p.shape[0]
    span = ATTN_SPAN
    sub = ATTN_BLOCK * dilation
    per = span // sub
    qb, kb, vb = [(off + g * A_OUT_WIDTH) // HEAD_DIM for off in (P_ATTN_Q, P_ATTN_K, P_ATTN_V)]

    def cur(off):
        return pl.BlockSpec((span, HEAD_DIM), lambda n, h: (n, off + h))

    def prev(off):
        return pl.BlockSpec((sub, HEAD_DIM), lambda n, h: (jnp.maximum(n * per - 1, 0), off + h))

    tab_c = pl.BlockSpec((span, HEAD_DIM), lambda n, h: (n, 0))
    tab_p = pl.BlockSpec((sub, HEAD_DIM), lambda n, h: (jnp.maximum(n * per - 1, 0), 0))
    out = pl.BlockSpec((span, HEAD_DIM), lambda n, h: (n, h))
    return pl.pallas_call(
        functools.partial(_attn_kernel, dilation=dilation),
        grid=(s // span, HEADS_PER_GROUP),
        in_specs=[cur(qb), cur(kb), cur(vb), prev(kb), prev(vb), tab_c, tab_c, tab_p, tab_p],
        out_specs=[out, out],
        out_shape=[jax.ShapeDtypeStruct((s, A_OUT_WIDTH), jnp.float32)] * 2,
        compiler_params=_params(_ARB2),
        name=f"attn_d{dilation}",
    )(p, p, p, p, p, cos, sin, cos, sin)


DN_PREP_COLS = 512
DN_HALO = 16
_QK_BLOCKS = 2 * DN_K_WIDTH // DN_PREP_COLS


def _dnprep_kernel(x_ref, halo_ref, w_ref, o_ref):
    i = pl.program_id(0)
    j = pl.program_id(1)
    x = x_ref[...].astype(jnp.float32)
    tm = x.shape[0]
    halo = jnp.where(i > 0, halo_ref[...].astype(jnp.float32), 0.0)
    xx = jnp.concatenate([halo, x], axis=0)
    w = w_ref[...]
    acc = x * w[DN_CONV - 1:DN_CONV, :]
    for sft in range(1, DN_CONV):
        acc = acc + xx[DN_HALO - sft:DN_HALO - sft + tm, :] * w[DN_CONV - 1 - sft:DN_CONV - sft, :]
    y = acc * _sigmoid(acc)
    for hh in range(DN_PREP_COLS // DN_DK):
        sl = slice(hh * DN_DK, (hh + 1) * DN_DK)
        yh = y[:, sl]
        f = lax.rsqrt(jnp.sum(yh * yh, axis=-1, keepdims=True) + NORM_EPS)
        f = jnp.where(j < _QK_BLOCKS, f, 1.0)
        o_ref[:, sl] = (yh * f).astype(o_ref.dtype)


def _dn_prep(p, conv_w, tm=512):
    s = p.shape[0]
    cb0 = P_DN_QKV // DN_PREP_COLS
    return pl.pallas_call(
        _dnprep_kernel,
        grid=(s // tm, DN_QKV_WIDTH // DN_PREP_COLS),
        in_specs=[pl.BlockSpec((tm, DN_PREP_COLS), lambda i, j: (i, cb0 + j)),
                  pl.BlockSpec((DN_HALO, DN_PREP_COLS),
                               lambda i, j: (jnp.maximum(i * (tm // DN_HALO) - 1, 0), cb0 + j)),
                  pl.BlockSpec((DN_CONV, DN_PREP_COLS), lambda i, j: (0, j))],
        out_specs=pl.BlockSpec((tm, DN_PREP_COLS), lambda i, j: (i, j)),
        out_shape=jax.ShapeDtypeStruct((s, DN_QKV_WIDTH), jnp.bfloat16),
        compiler_params=_params(_ARB2),
        name="dn_prep",
    )(p, p, conv_w)


DELTA_CHUNK = 128
DELTA_HG = 16


def _softplus(x):
    return jnp.maximum(x, 0.0) + jnp.log1p(jnp.exp(-jnp.abs(x)))


def _delta_local_kernel(q_ref, k_ref, v_ref, braw_ref, araw_ref, alog_ref, dtb_ref, u_ref, wq_ref, akt_ref, dl_ref):
    c = DELTA_CHUNK
    hi = lax.Precision.HIGHEST
    beta_t = _sigmoid(braw_ref[...])
    g_t = -jnp.exp(alog_ref[...]) * _softplus(araw_ref[...] + dtb_ref[...])

    ii = lax.broadcasted_iota(jnp.int32, (c, c), 0)
    jj = lax.broadcasted_iota(jnp.int32, (c, c), 1)
    lower = (ii >= jj).astype(jnp.float32)
    upper = (ii <= jj).astype(jnp.float32)
    eye = (ii == jj).astype(jnp.float32)
    scale = DN_DK ** -0.5

    gcum_row = jnp.dot(g_t, upper, precision=hi, preferred_element_type=jnp.float32)
    gcum_col = lax.dot_general(lower, g_t, (((1,), (1,)), ((), ())), precision=hi,
                               preferred_element_type=jnp.float32)
    beta_col = lax.dot_general(eye, beta_t, (((1,), (1,)), ((), ())), precision=hi,
                               preferred_element_type=jnp.float32)
    heads = range(DELTA_HG)
    cols = [slice(hh * DN_DK, (hh + 1) * DN_DK) for hh in heads]
    lms = []
    for hh in heads:
        q = q_ref[:, cols[hh]].astype(jnp.float32)
        k = k_ref[:, cols[hh]].astype(jnp.float32)
        gcol = gcum_col[:, hh:hh + 1]
        dec = jnp.exp(jnp.minimum(gcol - gcum_row[hh:hh + 1, :], 0.0))
        both = _dot_nt(jnp.concatenate([q, k], axis=0).astype(jnp.bfloat16), k.astype(jnp.bfloat16))
        g_last = gcol[c - 1:c, :]
        akt_ref[hh, :c, :] = (jnp.where(ii >= jj, both[:c] * dec, 0.0) * scale).astype(akt_ref.dtype)
        akt_ref[hh, c:, :] = (k * jnp.exp(g_last - gcol)).T.astype(akt_ref.dtype)
        wq_ref[hh, c:, :] = (q * (scale * jnp.exp(gcol))).astype(wq_ref.dtype)
        dl_ref[hh] = jnp.broadcast_to(jnp.exp(g_last), (1, DN_DK))
        lms.append(jnp.where(ii > jj, both[c:] * dec * beta_col[:, hh:hh + 1], 0.0))
    diff_bits = jnp.bitwise_xor(ii, jj)
    ts = [eye - jnp.where(diff_bits == 1, lm, 0.0) for lm in lms]
    level = 1
    while (2 << level) <= c:
        quad = jnp.right_shift(diff_bits, level) == 1
        offs = [jnp.where(quad, lm, 0.0).astype(jnp.bfloat16) for lm in lms]
        t16s = [t.astype(jnp.bfloat16) for t in ts]
        mos = [_dot(t16, off).astype(jnp.bfloat16) for t16, off in zip(t16s, offs)]
        ts = [t - _dot(mo, t16) for t, mo, t16 in zip(ts, mos, t16s)]
        level += 1
    for hh in heads:
        k = k_ref[:, cols[hh]].astype(jnp.float32)
        v = v_ref[:, cols[hh]].astype(jnp.float32)
        bcol = beta_col[:, hh:hh + 1]
        rhs = jnp.concatenate([v * bcol, k * (bcol * jnp.exp(gcum_col[:, hh:hh + 1]))], axis=1).astype(jnp.bfloat16)
        uw = _dot(ts[hh].astype(jnp.bfloat16), rhs)
        u_ref[:, cols[hh]] = uw[:, :DN_DK]
        wq_ref[hh, :c, :] = uw[:, DN_DK:].astype(wq_ref.dtype)


def _delta_scan_kernel(u_ref, wq_ref, akt_ref, dl_ref, z_ref, nw_ref, o_ref, state_ref):
    c = DELTA_CHUNK

    @pl.when(pl.program_id(0) == 0)
    def _():
        state_ref[...] = jnp.zeros(state_ref.shape, jnp.float32)

    nw = nw_ref[...]
    group = 4
    for h0 in range(0, DN_HEADS, group):
        hs = range(h0, h0 + group)
        cols = {h: slice(h * DN_DK, (h + 1) * DN_DK) for h in hs}
        wqs = {h: _dot(wq_ref[h], state_ref[h].astype(jnp.bfloat16)) for h in hs}
        rs = {h: _dot(akt_ref[h], (u_ref[:, cols[h]] - wqs[h][:c]).astype(jnp.bfloat16)) for h in hs}
        for h in hs:
            state_ref[h] = state_ref[h] * dl_ref[h] + rs[h][c:]
        for h in hs:
            z = z_ref[:, cols[h]].astype(jnp.float32)
            o_ref[:, cols[h]] = (_rms(wqs[h][c:] + rs[h][:c], nw) * (z * _sigmoid(z))).astype(o_ref.dtype)


def _delta(qkvn, p, small_t, a_log, dt_bias, norm_w):
    s = qkvn.shape[0]
    c = DELTA_CHUNK
    nc = s // c
    w = DELTA_HG * DN_DK
    nhg = DN_HEADS // DELTA_HG
    small_g = small_t.reshape(2 * nhg, DELTA_HG, s)
    a_log_g = a_log.reshape(nhg, DELTA_HG, 1)
    dt_bias_g = dt_bias.reshape(nhg, DELTA_HG, 1)

    def blk(off):
        return pl.BlockSpec((c, w), lambda i, j: (i, off + j))

    def head_rows(off):
        return pl.BlockSpec((None, DELTA_HG, c), lambda i, j: (off + j, 0, i))

    per_head = pl.BlockSpec((None, DELTA_HG, 1), lambda i, j: (j, 0, 0))
    stacked = pl.BlockSpec((None, DELTA_HG, 2 * c, DN_DK), lambda i, j: (i, j, 0, 0))
    u, wq, akt, dl = pl.pallas_call(
        _delta_local_kernel,
        grid=(nc, nhg),
        in_specs=[blk(0), blk(nhg), blk(2 * nhg), head_rows(0), head_rows(nhg), per_head, per_head],
        out_specs=[pl.BlockSpec((c, w), lambda i, j: (i, j)), stacked, stacked,
                   pl.BlockSpec((None, DELTA_HG, 1, DN_DK), lambda i, j: (i, j, 0, 0))],
        out_shape=[jax.ShapeDtypeStruct((s, DN_K_WIDTH), jnp.float32),
                   jax.ShapeDtypeStruct((nc, DN_HEADS, 2 * c, DN_DK), jnp.bfloat16),
                   jax.ShapeDtypeStruct((nc, DN_HEADS, 2 * c, DN_DK), jnp.bfloat16),
                   jax.ShapeDtypeStruct((nc, DN_HEADS, 1, DN_DK), jnp.float32)],
        compiler_params=_params(_ARB2),
        name="delta_local",
    )(qkvn, qkvn, qkvn, small_g, small_g, a_log_g, dt_bias_g)

    full = pl.BlockSpec((None, DN_HEADS, 2 * c, DN_DK), lambda i: (i, 0, 0, 0))
    return pl.pallas_call(
        _delta_scan_kernel,
        grid=(nc,),
        in_specs=[pl.BlockSpec((c, DN_K_WIDTH), lambda i: (i, 0)), full, full,
                  pl.BlockSpec((None, DN_HEADS, 1, DN_DK), lambda i: (i, 0, 0, 0)),
                  pl.BlockSpec((c, DN_K_WIDTH), lambda i: (i, P_DN_Z // DN_K_WIDTH)),
                  pl.BlockSpec((1, DN_DK), lambda i: (0, 0))],
        out_specs=pl.BlockSpec((c, DN_K_WIDTH), lambda i: (i, 0)),
        out_shape=jax.ShapeDtypeStruct((s, DN_K_WIDTH), jnp.bfloat16),
        scratch_shapes=[pltpu.VMEM((DN_HEADS, DN_DK, DN_DK), jnp.float32)],
        compiler_params=_params(_ARB1),
        name="delta_scan",
    )(u, wq, akt, dl, p, norm_w)


def _merge_kernel(o0, o1, o2, l0, l1, l2, yd_ref, woa_ref, wod_ref, ga_ref, gd_ref, out_ref, ya_ref):
    @pl.when(pl.program_id(1) == 0)
    def _():
        la, lb, lc = l0[...], l1[...], l2[...]
        m = jnp.maximum(jnp.maximum(la, lb), lc)
        ea, eb, ec = jnp.exp(la - m), jnp.exp(lb - m), jnp.exp(lc - m)
        ya = (o0[...] * ea + o1[...] * eb + o2[...] * ec) / (ea + eb + ec)
        ya_ref[...] = ya.astype(ya_ref.dtype)

    acc_a = _dot(ya_ref[...], woa_ref[...])
    acc_d = _dot(yd_ref[...], wod_ref[...])
    gate_a = _sigmoid(ga_ref[...].astype(jnp.float32))
    gate_d = _sigmoid(gd_ref[...].astype(jnp.float32))
    out_ref[...] = (gate_a * acc_a + gate_d * acc_d).astype(out_ref.dtype)


def _merge(os_, ls_, y_d, w_oa, w_od, p, tm=512, tn=1024):
    s = y_d.shape[0]
    n = w_oa.shape[1]
    att = pl.BlockSpec((tm, A_OUT_WIDTH), lambda i, j: (i, 0))
    gdb = P_GATE_D // tn
    return pl.pallas_call(
        _merge_kernel,
        grid=(s // tm, n // tn),
        in_specs=[att] * 6 + [
            pl.BlockSpec((tm, DN_K_WIDTH), lambda i, j: (i, 0)),
            pl.BlockSpec((A_OUT_WIDTH, tn), lambda i, j: (0, j)),
            pl.BlockSpec((DN_K_WIDTH, tn), lambda i, j: (0, j)),
            pl.BlockSpec((tm, tn), lambda i, j: (i, j)),
            pl.BlockSpec((tm, tn), lambda i, j: (i, gdb + j))],
        out_specs=pl.BlockSpec((tm, tn), lambda i, j: (i, j)),
        out_shape=jax.ShapeDtypeStruct((s, n), jnp.bfloat16),
        scratch_shapes=[pltpu.VMEM((tm, A_OUT_WIDTH), jnp.bfloat16)],
        compiler_params=_params(_ARB2),
        name="merge",
    )(*os_, *ls_, y_d, w_oa, w_od, p, p)


def _postmix_kernel(mix_ref, x_ref, gate_ref, pn_ref, fn_ref, shift_ref, scale_ref, x1_ref, h_ref):
    x1 = x_ref[...] + gate_ref[...] * _rms(mix_ref[...], pn_ref[...])
    x1_ref[...] = x1
    h_ref[...] = (_rms(x1, fn_ref[...]) * (1.0 + scale_ref[...]) + shift_ref[...]).astype(h_ref.dtype)


def _post_mix(mix, x, gate, post_w, ffn_w, shift, scale, tm=256):
    s, d = x.shape
    row = pl.BlockSpec((1, d), lambda i: (0, 0))
    blk = pl.BlockSpec((tm, d), lambda i: (i, 0))
    return pl.pallas_call(
        _postmix_kernel,
        grid=(s // tm,),
        in_specs=[blk, blk, row, row, row, row, row],
        out_specs=[blk, blk],
        out_shape=[jax.ShapeDtypeStruct((s, d), jnp.float32), jax.ShapeDtypeStruct((s, d), jnp.bfloat16)],
        compiler_params=_params(_ARB1),
        name="post_mix",
    )(mix, x, gate, post_w, ffn_w, shift, scale)


def _final_kernel(y_ref, x_ref, gate_ref, pn_ref, o_ref):
    o_ref[...] = x_ref[...] + gate_ref[...] * _rms(y_ref[...], pn_ref[...])


def _final(y, x1, gate, post_w, tm=256):
    s, d = x1.shape
    row = pl.BlockSpec((1, d), lambda i: (0, 0))
    blk = pl.BlockSpec((tm, d), lambda i: (i, 0))
    return pl.pallas_call(
        _final_kernel,
        grid=(s // tm,),
        in_specs=[blk, blk, row, row],
        out_specs=blk,
        out_shape=jax.ShapeDtypeStruct((s, d), jnp.float32),
        compiler_params=_params(_ARB1),
        name="final",
    )(y, x1, gate, post_w)


ROUTER_TM = 256


def _topk_desc(scores):
    work = scores
    vals = []
    for it in range(PEER_TOPK):
        mx = jnp.max(work, axis=0, keepdims=True)
        vals.append(mx)
        if it + 1 < PEER_TOPK:
            work = jnp.where(work == mx, -jnp.inf, work)
    return jnp.concatenate(vals, axis=0)


def _candidate_sums(v1, v2):
    half = PEER_TOPK // 2
    row = lax.broadcasted_iota(jnp.int32, (half, v1.shape[1]), 0)
    parts = [v1[0:1, :] + v2]
    for i in range(1, half):
        parts.append(jnp.where(row < PEER_TOPK // (i + 1), v1[i:i + 1, :] + v2[:half, :], -jnp.inf))
    parts.append(v1[half:, :] + v2[0:1, :])
    return jnp.concatenate(parts, axis=0)


def _router_kernel(q_ref, k1_ref, k2_ref, s1_ref, e1_ref, s2_ref, e2_ref, thr_ref):
    k1 = k1_ref[...].astype(jnp.bfloat16)
    k2 = k2_ref[...].astype(jnp.bfloat16)

    def head(h, carry):
        c0 = pl.multiple_of(h * 2 * PEER_HALF, 2 * PEER_HALF)
        q1 = q_ref[:, pl.ds(c0, PEER_HALF)].astype(jnp.bfloat16)
        q2 = q_ref[:, pl.ds(c0 + PEER_HALF, PEER_HALF)].astype(jnp.bfloat16)
        s1 = _dot_nt(k1, q1)
        s2 = _dot_nt(k2, q2)
        v1 = _topk_desc(s1)
        v2 = _topk_desc(s2)
        cand = _candidate_sums(v1, v2)
        thr = _topk_desc(cand)[PEER_TOPK - 1:PEER_TOPK, :]
        top = v1[0:1, :] + v2[0:1, :]
        zsum = jnp.sum(jnp.where(cand >= thr, jnp.exp(cand - top), 0.0), axis=0, keepdims=True)
        s1_ref[h] = s1
        e1_ref[h] = jnp.exp(s1 - v1[0:1, :]) / zsum
        s2_ref[h] = s2
        e2_ref[h] = jnp.exp(s2 - v2[0:1, :])
        thr_ref[pl.ds(h, 1), :] = thr
        return carry

    lax.fori_loop(0, PEER_HEADS, head, 0)


def _router(q, keys_1, keys_2):
    s = q.shape[0]
    tm = ROUTER_TM
    h_major = pl.BlockSpec((PEER_HEADS, PEER_KEYS, tm), lambda i: (0, 0, i))
    kspec = pl.BlockSpec((PEER_KEYS, PEER_HALF), lambda i: (0, 0))
    return pl.pallas_call(
        _router_kernel,
        grid=(s // tm,),
        in_specs=[pl.BlockSpec((tm, 2 * PEER_HALF * PEER_HEADS), lambda i: (i, 0)), kspec, kspec],
        out_specs=[h_major] * 4 + [pl.BlockSpec((PEER_HEADS, tm), lambda i: (0, i))],
        out_shape=[jax.ShapeDtypeStruct((PEER_HEADS, PEER_KEYS, s), jnp.float32)] * 4
        + [jax.ShapeDtypeStruct((PEER_HEADS, s), jnp.float32)],
        compiler_params=_params(_ARB1),
        name="router",
    )(q, keys_1, keys_2)


PEER_TM = 512
PEER_EC = 512
_A_PER_STEP = PEER_EC // PEER_KEYS


def _gelu(x):
    return 0.5 * x * (1.0 + lax.erf(x * (2.0 ** -0.5)))


_PEER_CHUNKS = PEER_EXPERTS // PEER_EC
_MXU_N = 256
_ACT_ROWS = 32
_SCORE_K = 256
_VALUE_K = 256


def _peer_kernel(h_ref, u_ref, v_ref, s1_ref, e1_ref, s2_ref, e2_ref, thr_ref, o_ref, sc0, sc1, act0, act1):
    step = pl.program_id(0)
    sc_ref, act_ref = (sc0, sc1), (act0, act1)

    @pl.when(step == 0)
    def _():
        sc0[...] = jnp.zeros(sc0.shape, sc0.dtype)
        act0[...] = jnp.zeros(act0.shape, act0.dtype)

    @pl.when(step > 0)
    def _():
        sc0[...] = sc1[...]
        act0[...] = act1[...]

    @pl.when(jnp.maximum(step - 2, 0) % _PEER_CHUNKS == 0)
    def _():
        o_ref[...] = jnp.zeros(o_ref.shape, jnp.float32)

    tm, d_model = h_ref.shape

    def stages(cur, nxt):
        def value_sum(nt, kt):
            dcol = slice(nt * _MXU_N, (nt + 1) * _MXU_N)
            ex = slice(kt * _VALUE_K, (kt + 1) * _VALUE_K)
            o_ref[:, dcol] += _dot_tn(act_ref[cur][ex, :], v_ref[ex, dcol])

        def activations(al, lt, bs):
            tok = slice(lt * LANES, (lt + 1) * LANES)
            keys2 = slice(bs * _ACT_ROWS, (bs + 1) * _ACT_ROWS)
            gate = jnp.zeros((_ACT_ROWS, LANES), jnp.float32)
            for h in range(PEER_HEADS):
                cand = s1_ref[al, h:h + 1, tok] + s2_ref[h, keys2, tok]
                w = e1_ref[al, h:h + 1, tok] * e2_ref[h, keys2, tok]
                gate = gate + jnp.where(cand >= thr_ref[h:h + 1, tok], w, 0.0)
            ex = slice(al * PEER_KEYS + bs * _ACT_ROWS, al * PEER_KEYS + (bs + 1) * _ACT_ROWS)
            act_ref[nxt][ex, tok] = (gate * _gelu(sc_ref[cur][ex, tok])).astype(act_ref[nxt].dtype)

        def scores(nt, kt):
            tok = slice(nt * _MXU_N, (nt + 1) * _MXU_N)
            ks = slice(kt * _SCORE_K, (kt + 1) * _SCORE_K)
            part = _dot_nt(u_ref[:, ks], h_ref[tok, ks])
            if kt == 0:
                sc_ref[nxt][:, tok] = part
            else:
                sc_ref[nxt][:, tok] += part

        s3 = [functools.partial(value_sum, nt, kt) for nt in range(d_model // _MXU_N)
              for kt in range(PEER_EC // _VALUE_K)]
        s2 = [functools.partial(activations, al, lt, bs) for al in range(_A_PER_STEP)
              for lt in range(tm // LANES) for bs in range(PEER_KEYS // _ACT_ROWS)]
        s1 = [functools.partial(scores, nt, kt) for nt in range(tm // _MXU_N) for kt in range(d_model // _SCORE_K)]
        rounds = max(len(s3), len(s2))
        for k in range(rounds):
            for pieces in (s3, s2, s1):
                lo, hi = k * len(pieces) // rounds, (k + 1) * len(pieces) // rounds
                for piece in pieces[lo:hi]:
                    piece()

    stages(0, 1)


def _peer(h2, u16, v16, s1, e1, s2, e2, thr):
    s, d = h2.shape
    tm, ec = PEER_TM, PEER_EC
    npairs = (s // tm) * _PEER_CHUNKS

    def pair(step, lag):
        idx = jnp.clip(step - lag, 0, npairs - 1)
        return idx // _PEER_CHUNKS, idx % _PEER_CHUNKS

    def tile(lag):
        return lambda t: (pair(t, lag)[0], 0)

    def chunk(lag):
        return lambda t: (pair(t, lag)[1], 0)

    a_major = pl.BlockSpec((_A_PER_STEP, PEER_HEADS, tm), lambda t: (pair(t, 1)[1], 0, pair(t, 1)[0]))
    h_major = pl.BlockSpec((PEER_HEADS, PEER_KEYS, tm), lambda t: (0, 0, pair(t, 1)[0]))
    return pl.pallas_call(
        _peer_kernel,
        grid=(npairs + 2,),
        in_specs=[pl.BlockSpec((tm, d), tile(0)),
                  pl.BlockSpec((ec, d), chunk(0)),
                  pl.BlockSpec((ec, d), chunk(2)),
                  a_major, a_major, h_major, h_major,
                  pl.BlockSpec((PEER_HEADS, tm), lambda t: (0, pair(t, 1)[0]))],
        out_specs=pl.BlockSpec((tm, d), tile(2)),
        out_shape=jax.ShapeDtypeStruct((s, d), jnp.float32),
        scratch_shapes=[pltpu.VMEM((ec, tm), jnp.float32)] * 2 + [pltpu.VMEM((ec, tm), jnp.bfloat16)] * 2,
        compiler_params=_params(_ARB1),
        name="peer",
    )(h2, u16, v16, s1, e1, s2, e2, thr)


def _rope_tables(positions):
    half = HEAD_DIM // 2
    inv_freq = ROPE_THETA ** (-jnp.arange(half, dtype=jnp.float32) / half)
    ang = positions.astype(jnp.float32)[:, None] * inv_freq[None, :]
    cos, sin = jnp.cos(ang), jnp.sin(ang)
    return jnp.concatenate([cos, cos], axis=-1), jnp.concatenate([-sin, sin], axis=-1)


def _layer(x, c_col, positions, w_ada, b_ada, attn_pre_norm, attn_post_norm, w_in, conv_w, a_log, dt_bias,
           dn_norm_w, w_o_attn, w_o_dn, w_out, ffn_pre_norm, ffn_post_norm, w_peer_q, peer_keys_1,
           peer_keys_2, expert_u, expert_v):
    d = D_MODEL
    bf = jnp.bfloat16
    mod = _adaln(c_col, w_ada, b_ada[None, :])
    shift_mix, scale_mix, gate_mix, shift_ffn, scale_ffn, gate_ffn = [mod[:, i * d:(i + 1) * d] for i in range(6)]

    h = _prenorm(x, attn_pre_norm[None, :], shift_mix, scale_mix)
    w_in16 = w_in.astype(bf)
    p_attn = _matmul_cols(h, w_in16, W_ATTN, W_DN - W_ATTN, jnp.float32, 1024, 512, "in_proj_attn")
    p_dn = _matmul_cols(h, w_in16, W_DN, W_SMALL - W_DN, bf, 1024, 512, "in_proj_dn")
    p_gates = _matmul(h, w_in16[:, W_GATES:], bf, 1024, 512, "in_proj_gates")
    w_small = jnp.pad(w_in16[:, W_SMALL:W_GATES], ((0, 0), (0, LANES - 2 * DN_HEADS)))
    small_t = _small_proj(h, w_small)

    cos, sin = _rope_tables(positions)
    outs, lses = [], []
    for g, (_, dilation) in enumerate(DILATED_GROUPS):
        o, lse = _attention_group(p_attn, cos, sin, g, dilation)
        outs.append(o)
        lses.append(lse)

    qkvn = _dn_prep(p_dn, conv_w)
    y_d = _delta(qkvn, p_dn, small_t, a_log, dt_bias, dn_norm_w[None, :])

    merged = _merge(outs, lses, y_d, w_o_attn.astype(bf), w_o_dn.astype(bf), p_gates)
    mix = _matmul(merged, w_out.astype(bf), jnp.float32, 1024, 512, "out_proj")
    x1, h2 = _post_mix(mix, x, gate_mix, attn_post_norm[None, :], ffn_pre_norm[None, :], shift_ffn, scale_ffn)

    q = _matmul(h2, w_peer_q.astype(bf), jnp.float32, 1024, 512, "peer_q")
    s1, e1, s2, e2, thr = _router(q, peer_keys_1, peer_keys_2)
    s1, e1 = jnp.transpose(s1, (1, 0, 2)), jnp.transpose(e1, (1, 0, 2))
    y = _peer(h2, expert_u.astype(bf), expert_v.astype(bf), s1, e1, s2, e2, thr)
    return _final(y, x1, gate_ffn, ffn_post_norm[None, :])


def kernel(x, c, positions, w_ada, b_ada, attn_pre_norm, attn_post_norm, w_in, conv_w, a_log, dt_bias, dn_norm_w,
           w_o_attn, w_o_dn, w_out, ffn_pre_norm, ffn_post_norm, w_peer_q, peer_keys_1, peer_keys_2, expert_u,
           expert_v):
    batch, seq, d = x.shape
    depth = w_ada.shape[0]
    outs = []
    for b in range(batch):
        xb = x[b]
        c_col = c[b][:, None]
        for l in range(depth):
            xb = _layer(xb, c_col, positions[b], w_ada[l], b_ada[l], attn_pre_norm[l], attn_post_norm[l], w_in[l],
                        conv_w[l], a_log[l], dt_bias[l], dn_norm_w[l], w_o_attn[l], w_o_dn[l], w_out[l],
                        ffn_pre_norm[l], ffn_post_norm[l], w_peer_q[l], peer_keys_1[l], peer_keys_2[l],
                        expert_u[l], expert_v[l])
        outs.append(xb)
    return jnp.stack(outs, axis=0)
```

```python
import functools

import jax
import jax.numpy as jnp
from jax import lax
from jax.experimental import pallas as pl
from jax.experimental.pallas import tpu as pltpu

D_MODEL = 4096
NORM_EPS = 1e-6
NEG_INF = -1e30

HEAD_DIM = 128
ROPE_THETA = 10000.0
DILATED_GROUPS = ((128, 1), (512, 4), (2048, 16))
HEADS_PER_GROUP = 4
A_WIDTH = 1536
A_OUT_WIDTH = HEADS_PER_GROUP * HEAD_DIM
ATTN_BLOCK = 128
ATTN_SPAN = 2048

DN_HEADS = 16
DN_DK = 128
DN_CONV = 4
DN_K_WIDTH = DN_HEADS * DN_DK
DN_QKV_WIDTH = 3 * DN_K_WIDTH

PEER_HEADS = 8
PEER_KEYS = 128
PEER_EXPERTS = PEER_KEYS * PEER_KEYS
PEER_HALF = 128
PEER_TOPK = 16

P_ATTN_Q = 0
P_ATTN_K = A_WIDTH
P_ATTN_V = 2 * A_WIDTH
P_DN_QKV = 0
P_DN_Z = DN_QKV_WIDTH
P_GATE_A = 0
P_GATE_D = D_MODEL
W_ATTN = 0
W_DN = 3 * A_WIDTH
W_Z = W_DN + DN_QKV_WIDTH
W_SMALL = W_Z + DN_K_WIDTH
W_GATES = W_SMALL + 2 * DN_HEADS

LANES = 128
VMEM_LIMIT = 52 * 1024 * 1024

_ARB1 = ("arbitrary",)
_ARB2 = ("arbitrary", "arbitrary")


def _params(sem):
    return pltpu.CompilerParams(dimension_semantics=sem, vmem_limit_bytes=VMEM_LIMIT)


def _sigmoid(x):
    return 1.0 / (1.0 + jnp.exp(-x))


def _dot(a, b):
    return jnp.dot(a, b, preferred_element_type=jnp.float32)


def _dot_nt(a, b):
    return lax.dot_general(a, b, (((1,), (1,)), ((), ())), preferred_element_type=jnp.float32)


def _dot_tn(a, b):
    return lax.dot_general(a, b, (((0,), (0,)), ((), ())), preferred_element_type=jnp.float32)


def _rms(x, w):
    return x * lax.rsqrt(jnp.mean(x * x, axis=-1, keepdims=True) + NORM_EPS) * w


def _adaln_kernel(c_ref, w_ref, b_ref, o_ref):
    c = c_ref[...]
    act = c * _sigmoid(c)
    o_ref[...] = jnp.sum(act * w_ref[...], axis=0, keepdims=True) + b_ref[...]


def _adaln(c_col, w, b, tn=512):
    d, n = w.shape
    return pl.pallas_call(
        _adaln_kernel,
        grid=(n // tn,),
        in_specs=[pl.BlockSpec((d, 1), lambda j: (0, 0)),
                  pl.BlockSpec((d, tn), lambda j: (0, j)),
                  pl.BlockSpec((1, tn), lambda j: (0, j))],
        out_specs=pl.BlockSpec((1, tn), lambda j: (0, j)),
        out_shape=jax.ShapeDtypeStruct((1, n), jnp.float32),
        compiler_params=_params(_ARB1),
        name="adaln",
    )(c_col, w, b)


def _prenorm_kernel(x_ref, nw_ref, shift_ref, scale_ref, o_ref):
    y = _rms(x_ref[...], nw_ref[...])
    o_ref[...] = (y * (1.0 + scale_ref[...]) + shift_ref[...]).astype(o_ref.dtype)


def _prenorm(x, nw, shift, scale, tm=256):
    s, d = x.shape
    row = pl.BlockSpec((1, d), lambda i: (0, 0))
    return pl.pallas_call(
        _prenorm_kernel,
        grid=(s // tm,),
        in_specs=[pl.BlockSpec((tm, d), lambda i: (i, 0)), row, row, row],
        out_specs=pl.BlockSpec((tm, d), lambda i: (i, 0)),
        out_shape=jax.ShapeDtypeStruct((s, d), jnp.bfloat16),
        compiler_params=_params(_ARB1),
        name="prenorm",
    )(x, nw, shift, scale)


def _mm_kernel(a_ref, b_ref, o_ref):
    o_ref[...] = _dot(a_ref[...], b_ref[...]).astype(o_ref.dtype)


def _matmul(a, b, out_dtype, tm, tn, name):
    m, k = a.shape
    n = b.shape[1]
    return pl.pallas_call(
        _mm_kernel,
        grid=(m // tm, n // tn),
        in_specs=[pl.BlockSpec((tm, k), lambda i, j: (i, 0)),
                  pl.BlockSpec((k, tn), lambda i, j: (0, j))],
        out_specs=pl.BlockSpec((tm, tn), lambda i, j: (i, j)),
        out_shape=jax.ShapeDtypeStruct((m, n), out_dtype),
        compiler_params=_params(_ARB2),
        name=name,
    )(a, b)


def _matmul_cols(a, b, col0, n, out_dtype, tm, tn, name):
    m, k = a.shape
    cb0 = col0 // tn
    return pl.pallas_call(
        _mm_kernel,
        grid=(m // tm, n // tn),
        in_specs=[pl.BlockSpec((tm, k), lambda i, j: (i, 0)),
                  pl.BlockSpec((k, tn), lambda i, j: (0, cb0 + j))],
        out_specs=pl.BlockSpec((tm, tn), lambda i, j: (i, j)),
        out_shape=jax.ShapeDtypeStruct((m, n), out_dtype),
        compiler_params=_params(_ARB2),
        name=name,
    )(a, b)


def _small_kernel(h_ref, w_ref, ot_ref):
    ot_ref[...] = _dot(h_ref[...], w_ref[...]).T[:ot_ref.shape[0], :]


def _small_proj(h, w_pad, tm=512):
    s, d = h.shape
    r = 2 * DN_HEADS
    return pl.pallas_call(
        _small_kernel,
        grid=(s // tm,),
        in_specs=[pl.BlockSpec((tm, d), lambda i: (i, 0)),
                  pl.BlockSpec((d, LANES), lambda i: (0, 0))],
        out_specs=pl.BlockSpec((r, tm), lambda i: (0, i)),
        out_shape=jax.ShapeDtypeStruct((r, s), jnp.float32),
        compiler_params=_params(_ARB1),
        name="small_proj",
    )(h, w_pad)


def _rope(x, cos, sin):
    return x * cos + pltpu.roll(x, HEAD_DIM // 2, 1) * sin


def _attn_kernel(qc_ref, kc_ref, vc_ref, kp_ref, vp_ref, cc_ref, sc_ref, cp_ref, sp_ref, o_ref, l_ref, *, dilation):
    n = pl.program_id(0)
    sub = ATTN_BLOCK * dilation
    qi = lax.broadcasted_iota(jnp.int32, (ATTN_BLOCK, ATTN_BLOCK), 0)
    kj = lax.broadcasted_iota(jnp.int32, (ATTN_BLOCK, ATTN_BLOCK), 1)
    mask_c = kj <= qi
    mask_in = kj >= qi
    mask_first = jnp.logical_and(mask_in, n > 0)
    scale = HEAD_DIM ** -0.5

    def rows_of(sb, r):
        return pl.ds(sb * sub + r, ATTN_BLOCK, stride=dilation) if dilation > 1 else pl.ds(sb * sub, ATTN_BLOCK)

    def scores(sb, r):
        rows = rows_of(sb, r)
        cos_c, sin_c = cc_ref[rows, :], sc_ref[rows, :]
        if sb == 0:
            prow = rows_of(0, r)
            k_prev, v_prev, cos_p, sin_p, mask_p = kp_ref[prow, :], vp_ref[prow, :], cp_ref[prow, :], sp_ref[prow, :], mask_first
        else:
            prow = rows_of(sb - 1, r)
            k_prev, v_prev, cos_p, sin_p, mask_p = kc_ref[prow, :], vc_ref[prow, :], cc_ref[prow, :], sc_ref[prow, :], mask_in
        q = _rope(qc_ref[rows, :], cos_c, sin_c).astype(jnp.bfloat16)
        k_c = _rope(kc_ref[rows, :], cos_c, sin_c).astype(jnp.bfloat16)
        k_p = _rope(k_prev, cos_p, sin_p).astype(jnp.bfloat16)
        s_c = jnp.where(mask_c, _dot_nt(q, k_c) * scale, NEG_INF)
        s_p = jnp.where(mask_p, _dot_nt(q, k_p) * scale, NEG_INF)
        return rows, s_c, s_p, v_prev

    def finish(rows, s_c, s_p, v_prev):
        m = jnp.maximum(jnp.max(s_c, axis=-1, keepdims=True), jnp.max(s_p, axis=-1, keepdims=True))
        p_c = jnp.exp(s_c - m)
        p_p = jnp.exp(s_p - m)
        den = jnp.sum(p_c, axis=-1, keepdims=True) + jnp.sum(p_p, axis=-1, keepdims=True)
        pv = (_dot(p_c.astype(jnp.bfloat16), vc_ref[rows, :].astype(jnp.bfloat16))
              + _dot(p_p.astype(jnp.bfloat16), v_prev.astype(jnp.bfloat16)))
        o_ref[rows, :] = pv / den
        l_ref[rows, :] = jnp.broadcast_to(m + jnp.log(den), (ATTN_BLOCK, HEAD_DIM))

    items = [(sb, r) for sb in range(ATTN_SPAN // sub) for r in range(dilation)]
    batch = 4
    for b0 in range(0, len(items), batch):
        staged = [scores(sb, r) for sb, r in items[b0:b0 + batch]]
        for st in staged:
            finish(*st)


def _attention_group(p, cos, sin, g, dilation):
    s = p.shape[0]
    span = ATTN_SPAN
    sub = ATTN_BLOCK * dilation
    per = span // sub
    qb, kb, vb = [(off + g * A_OUT_WIDTH) // HEAD_DIM for off in (P_ATTN_Q, P_ATTN_K, P_ATTN_V)]

    def cur(off):
        return pl.BlockSpec((span, HEAD_DIM), lambda n, h: (n, off + h))

    def prev(off):
        return pl.BlockSpec((sub, HEAD_DIM), lambda n, h: (jnp.maximum(n * per - 1, 0), off + h))

    tab_c = pl.BlockSpec((span, HEAD_DIM), lambda n, h: (n, 0))
    tab_p = pl.BlockSpec((sub, HEAD_DIM), lambda n, h: (jnp.maximum(n * per - 1, 0), 0))
    out = pl.BlockSpec((span, HEAD_DIM), lambda n, h: (n, h))
    return pl.pallas_call(
        functools.partial(_attn_kernel, dilation=dilation),
        grid=(s // span, HEADS_PER_GROUP),
        in_specs=[cur(qb), cur(kb), cur(vb), prev(kb), prev(vb), tab_c, tab_c, tab_p, tab_p],
        out_specs=[out, out],
        out_shape=[jax.ShapeDtypeStruct((s, A_OUT_WIDTH), jnp.float32)] * 2,
        compiler_params=_params(_ARB2),
        name=f"attn_d{dilation}",
    )(p, p, p, p, p, cos, sin, cos, sin)


DN_PREP_COLS = 512
DN_HALO = 16
_QK_BLOCKS = 2 * DN_K_WIDTH // DN_PREP_COLS


def _dnprep_kernel(x_ref, halo_ref, w_ref, o_ref):
    i = pl.program_id(0)
    j = pl.program_id(1)
    x = x_ref[...].astype(jnp.float32)
    tm = x.shape[0]
    halo = jnp.where(i > 0, halo_ref[...].astype(jnp.float32), 0.0)
    xx = jnp.concatenate([halo, x], axis=0)
    w = w_ref[...]
    acc = x * w[DN_CONV - 1:DN_CONV, :]
    for sft in range(1, DN_CONV):
        acc = acc + xx[DN_HALO - sft:DN_HALO - sft + tm, :] * w[DN_CONV - 1 - sft:DN_CONV - sft, :]
    y = acc * _sigmoid(acc)
    for hh in range(DN_PREP_COLS // DN_DK):
        sl = slice(hh * DN_DK, (hh + 1) * DN_DK)
        yh = y[:, sl]
        f = lax.rsqrt(jnp.sum(yh * yh, axis=-1, keepdims=True) + NORM_EPS)
        f = jnp.where(j < _QK_BLOCKS, f, 1.0)
        o_ref[:, sl] = (yh * f).astype(o_ref.dtype)


def _dn_prep(p, conv_w, tm=512):
    s = p.shape[0]
    cb0 = P_DN_QKV // DN_PREP_COLS
    return pl.pallas_call(
        _dnprep_kernel,
        grid=(s // tm, DN_QKV_WIDTH // DN_PREP_COLS),
        in_specs=[pl.BlockSpec((tm, DN_PREP_COLS), lambda i, j: (i, cb0 + j)),
                  pl.BlockSpec((DN_HALO, DN_PREP_COLS),
                               lambda i, j: (jnp.maximum(i * (tm // DN_HALO) - 1, 0), cb0 + j)),
                  pl.BlockSpec((DN_CONV, DN_PREP_COLS), lambda i, j: (0, j))],
        out_specs=pl.BlockSpec((tm, DN_PREP_COLS), lambda i, j: (i, j)),
        out_shape=jax.ShapeDtypeStruct((s, DN_QKV_WIDTH), jnp.bfloat16),
        compiler_params=_params(_ARB2),
        name="dn_prep",
    )(p, p, conv_w)


DELTA_CHUNK = 128
DELTA_HG = 16


def _softplus(x):
    return jnp.maximum(x, 0.0) + jnp.log1p(jnp.exp(-jnp.abs(x)))


def _delta_local_kernel(q_ref, k_ref, v_ref, braw_ref, araw_ref, alog_ref, dtb_ref, u_ref, wq_ref, akt_ref, dl_ref):
    c = DELTA_CHUNK
    hi = lax.Precision.HIGHEST
    beta_t = _sigmoid(braw_ref[...])
    g_t = -jnp.exp(alog_ref[...]) * _softplus(araw_ref[...] + dtb_ref[...])

    ii = lax.broadcasted_iota(jnp.int32, (c, c), 0)
    jj = lax.broadcasted_iota(jnp.int32, (c, c), 1)
    lower = (ii >= jj).astype(jnp.float32)
    upper = (ii <= jj).astype(jnp.float32)
    eye = (ii == jj).astype(jnp.float32)
    scale = DN_DK ** -0.5

    gcum_row = jnp.dot(g_t, upper, precision=hi, preferred_element_type=jnp.float32)
    gcum_col = lax.dot_general(lower, g_t, (((1,), (1,)), ((), ())), precision=hi,
                               preferred_element_type=jnp.float32)
    beta_col = lax.dot_general(eye, beta_t, (((1,), (1,)), ((), ())), precision=hi,
                               preferred_element_type=jnp.float32)
    heads = range(DELTA_HG)
    cols = [slice(hh * DN_DK, (hh + 1) * DN_DK) for hh in heads]
    lms = []
    for hh in heads:
        q = q_ref[:, cols[hh]].astype(jnp.float32)
        k = k_ref[:, cols[hh]].astype(jnp.float32)
        gcol = gcum_col[:, hh:hh + 1]
        dec = jnp.exp(jnp.minimum(gcol - gcum_row[hh:hh + 1, :], 0.0))
        both = _dot_nt(jnp.concatenate([q, k], axis=0).astype(jnp.bfloat16), k.astype(jnp.bfloat16))
        g_last = gcol[c - 1:c, :]
        akt_ref[hh, :c, :] = (jnp.where(ii >= jj, both[:c] * dec, 0.0) * scale).astype(akt_ref.dtype)
        akt_ref[hh, c:, :] = (k * jnp.exp(g_last - gcol)).T.astype(akt_ref.dtype)
        wq_ref[hh, c:, :] = (q * (scale * jnp.exp(gcol))).astype(wq_ref.dtype)
        dl_ref[hh] = jnp.broadcast_to(jnp.exp(g_last), (1, DN_DK))
        lms.append(jnp.where(ii > jj, both[c:] * dec * beta_col[:, hh:hh + 1], 0.0))
    diff_bits = jnp.bitwise_xor(ii, jj)
    ts = [eye - jnp.where(diff_bits == 1, lm, 0.0) for lm in lms]
    level = 1
    while (2 << level) <= c:
        quad = jnp.right_shift(diff_bits, level) == 1
        offs = [jnp.where(quad, lm, 0.0).astype(jnp.bfloat16) for lm in lms]
        t16s = [t.astype(jnp.bfloat16) for t in ts]
        mos = [_dot(t16, off).astype(jnp.bfloat16) for t16, off in zip(t16s, offs)]
        ts = [t - _dot(mo, t16) for t, mo, t16 in zip(ts, mos, t16s)]
        level += 1
    for hh in heads:
        k = k_ref[:, cols[hh]].astype(jnp.float32)
        v = v_ref[:, cols[hh]].astype(jnp.float32)
        bcol = beta_col[:, hh:hh + 1]
        rhs = jnp.concatenate([v * bcol, k * (bcol * jnp.exp(gcum_col[:, hh:hh + 1]))], axis=1).astype(jnp.bfloat16)
        uw = _dot(ts[hh].astype(jnp.bfloat16), rhs)
        u_ref[:, cols[hh]] = uw[:, :DN_DK]
        wq_ref[hh, :c, :] = uw[:, DN_DK:].astype(wq_ref.dtype)


def _delta_scan_kernel(u_ref, wq_ref, akt_ref, dl_ref, z_ref, nw_ref, o_ref, state_ref):
    c = DELTA_CHUNK

    @pl.when(pl.program_id(0) == 0)
    def _():
        state_ref[...] = jnp.zeros(state_ref.shape, jnp.float32)

    nw = nw_ref[...]
    group = 4
    for h0 in range(0, DN_HEADS, group):
        hs = range(h0, h0 + group)
        cols = {h: slice(h * DN_DK, (h + 1) * DN_DK) for h in hs}
        wqs = {h: _dot(wq_ref[h], state_ref[h].astype(jnp.bfloat16)) for h in hs}
        rs = {h: _dot(akt_ref[h], (u_ref[:, cols[h]] - wqs[h][:c]).astype(jnp.bfloat16)) for h in hs}
        for h in hs:
            state_ref[h] = state_ref[h] * dl_ref[h] + rs[h][c:]
        for h in hs:
            z = z_ref[:, cols[h]].astype(jnp.float32)
            o_ref[:, cols[h]] = (_rms(wqs[h][c:] + rs[h][:c], nw) * (z * _sigmoid(z))).astype(o_ref.dtype)


def _delta(qkvn, p, small_t, a_log, dt_bias, norm_w):
    s = qkvn.shape[0]
    c = DELTA_CHUNK
    nc = s // c
    w = DELTA_HG * DN_DK
    nhg = DN_HEADS // DELTA_HG
    small_g = small_t.reshape(2 * nhg, DELTA_HG, s)
    a_log_g = a_log.reshape(nhg, DELTA_HG, 1)
    dt_bias_g = dt_bias.reshape(nhg, DELTA_HG, 1)

    def blk(off):
        return pl.BlockSpec((c, w), lambda i, j: (i, off + j))

    def head_rows(off):
        return pl.BlockSpec((None, DELTA_HG, c), lambda i, j: (off + j, 0, i))

    per_head = pl.BlockSpec((None, DELTA_HG, 1), lambda i, j: (j, 0, 0))
    stacked = pl.BlockSpec((None, DELTA_HG, 2 * c, DN_DK), lambda i, j: (i, j, 0, 0))
    u, wq, akt, dl = pl.pallas_call(
        _delta_local_kernel,
        grid=(nc, nhg),
        in_specs=[blk(0), blk(nhg), blk(2 * nhg), head_rows(0), head_rows(nhg), per_head, per_head],
        out_specs=[pl.BlockSpec((c, w), lambda i, j: (i, j)), stacked, stacked,
                   pl.BlockSpec((None, DELTA_HG, 1, DN_DK), lambda i, j: (i, j, 0, 0))],
        out_shape=[jax.ShapeDtypeStruct((s, DN_K_WIDTH), jnp.float32),
                   jax.ShapeDtypeStruct((nc, DN_HEADS, 2 * c, DN_DK), jnp.bfloat16),
                   jax.ShapeDtypeStruct((nc, DN_HEADS, 2 * c, DN_DK), jnp.bfloat16),
                   jax.ShapeDtypeStruct((nc, DN_HEADS, 1, DN_DK), jnp.float32)],
        compiler_params=_params(_ARB2),
        name="delta_local",
    )(qkvn, qkvn, qkvn, small_g, small_g, a_log_g, dt_bias_g)

    full = pl.BlockSpec((None, DN_HEADS, 2 * c, DN_DK), lambda i: (i, 0, 0, 0))
    return pl.pallas_call(
        _delta_scan_kernel,
        grid=(nc,),
        in_specs=[pl.BlockSpec((c, DN_K_WIDTH), lambda i: (i, 0)), full, full,
                  pl.BlockSpec((None, DN_HEADS, 1, DN_DK), lambda i: (i, 0, 0, 0)),
                  pl.BlockSpec((c, DN_K_WIDTH), lambda i: (i, P_DN_Z // DN_K_WIDTH)),
                  pl.BlockSpec((1, DN_DK), lambda i: (0, 0))],
        out_specs=pl.BlockSpec((c, DN_K_WIDTH), lambda i: (i, 0)),
        out_shape=jax.ShapeDtypeStruct((s, DN_K_WIDTH), jnp.bfloat16),
        scratch_shapes=[pltpu.VMEM((DN_HEADS, DN_DK, DN_DK), jnp.float32)],
        compiler_params=_params(_ARB1),
        name="delta_scan",
    )(u, wq, akt, dl, p, norm_w)


def _merge_kernel(o0, o1, o2, l0, l1, l2, yd_ref, woa_ref, wod_ref, ga_ref, gd_ref, out_ref, ya_ref):
    @pl.when(pl.program_id(1) == 0)
    def _():
        la, lb, lc = l0[...], l1[...], l2[...]
        m = jnp.maximum(jnp.maximum(la, lb), lc)
        ea, eb, ec = jnp.exp(la - m), jnp.exp(lb - m), jnp.exp(lc - m)
        ya = (o0[...] * ea + o1[...] * eb + o2[...] * ec) / (ea + eb + ec)
        ya_ref[...] = ya.astype(ya_ref.dtype)

    acc_a = _dot(ya_ref[...], woa_ref[...])
    acc_d = _dot(yd_ref[...], wod_ref[...])
    gate_a = _sigmoid(ga_ref[...].astype(jnp.float32))
    gate_d = _sigmoid(gd_ref[...].astype(jnp.float32))
    out_ref[...] = (gate_a * acc_a + gate_d * acc_d).astype(out_ref.dtype)


def _merge(os_, ls_, y_d, w_oa, w_od, p, tm=512, tn=1024):
    s = y_d.shape[0]
    n = w_oa.shape[1]
    att = pl.BlockSpec((tm, A_OUT_WIDTH), lambda i, j: (i, 0))
    gdb = P_GATE_D // tn
    return pl.pallas_call(
        _merge_kernel,
        grid=(s // tm, n // tn),
        in_specs=[att] * 6 + [
            pl.BlockSpec((tm, DN_K_WIDTH), lambda i, j: (i, 0)),
            pl.BlockSpec((A_OUT_WIDTH, tn), lambda i, j: (0, j)),
            pl.BlockSpec((DN_K_WIDTH, tn), lambda i, j: (0, j)),
            pl.BlockSpec((tm, tn), lambda i, j: (i, j)),
            pl.BlockSpec((tm, tn), lambda i, j: (i, gdb + j))],
        out_specs=pl.BlockSpec((tm, tn), lambda i, j: (i, j)),
        out_shape=jax.ShapeDtypeStruct((s, n), jnp.bfloat16),
        scratch_shapes=[pltpu.VMEM((tm, A_OUT_WIDTH), jnp.bfloat16)],
        compiler_params=_params(_ARB2),
        name="merge",
    )(*os_, *ls_, y_d, w_oa, w_od, p, p)


def _postmix_kernel(mix_ref, x_ref, gate_ref, pn_ref, fn_ref, shift_ref, scale_ref, x1_ref, h_ref):
    x1 = x_ref[...] + gate_ref[...] * _rms(mix_ref[...], pn_ref[...])
    x1_ref[...] = x1
    h_ref[...] = (_rms(x1, fn_ref[...]) * (1.0 + scale_ref[...]) + shift_ref[...]).astype(h_ref.dtype)


def _post_mix(mix, x, gate, post_w, ffn_w, shift, scale, tm=256):
    s, d = x.shape
    row = pl.BlockSpec((1, d), lambda i: (0, 0))
    blk = pl.BlockSpec((tm, d), lambda i: (i, 0))
    return pl.pallas_call(
        _postmix_kernel,
        grid=(s // tm,),
        in_specs=[blk, blk, row, row, row, row, row],
        out_specs=[blk, blk],
        out_shape=[jax.ShapeDtypeStruct((s, d), jnp.float32), jax.ShapeDtypeStruct((s, d), jnp.bfloat16)],
        compiler_params=_params(_ARB1),
        name="post_mix",
    )(mix, x, gate, post_w, ffn_w, shift, scale)


def _final_kernel(y_ref, x_ref, gate_ref, pn_ref, o_ref):
    o_ref[...] = x_ref[...] + gate_ref[...] * _rms(y_ref[...], pn_ref[...])


def _final(y, x1, gate, post_w, tm=256):
    s, d = x1.shape
    row = pl.BlockSpec((1, d), lambda i: (0, 0))
    blk = pl.BlockSpec((tm, d), lambda i: (i, 0))
    return pl.pallas_call(
        _final_kernel,
        grid=(s // tm,),
        in_specs=[blk, blk, row, row],
        out_specs=blk,
        out_shape=jax.ShapeDtypeStruct((s, d), jnp.float32),
        compiler_params=_params(_ARB1),
        name="final",
    )(y, x1, gate, post_w)


ROUTER_TM = 256


def _topk_desc(scores):
    work = scores
    vals = []
    for it in range(PEER_TOPK):
        mx = jnp.max(work, axis=0, keepdims=True)
        vals.append(mx)
        if it + 1 < PEER_TOPK:
            work = jnp.where(work == mx, -jnp.inf, work)
    return jnp.concatenate(vals, axis=0)


def _candidate_sums(v1, v2):
    half = PEER_TOPK // 2
    row = lax.broadcasted_iota(jnp.int32, (half, v1.shape[1]), 0)
    parts = [v1[0:1, :] + v2]
    for i in range(1, half):
        parts.append(jnp.where(row < PEER_TOPK // (i + 1), v1[i:i + 1, :] + v2[:half, :], -jnp.inf))
    parts.append(v1[half:, :] + v2[0:1, :])
    return jnp.concatenate(parts, axis=0)


def _router_kernel(q_ref, k1_ref, k2_ref, s1_ref, e1_ref, s2_ref, e2_ref, thr_ref):
    k1 = k1_ref[...].astype(jnp.bfloat16)
    k2 = k2_ref[...].astype(jnp.bfloat16)

    def head(h, carry):
        c0 = pl.multiple_of(h * 2 * PEER_HALF, 2 * PEER_HALF)
        q1 = q_ref[:, pl.ds(c0, PEER_HALF)].astype(jnp.bfloat16)
        q2 = q_ref[:, pl.ds(c0 + PEER_HALF, PEER_HALF)].astype(jnp.bfloat16)
        s1 = _dot_nt(k1, q1)
        s2 = _dot_nt(k2, q2)
        v1 = _topk_desc(s1)
        v2 = _topk_desc(s2)
        cand = _candidate_sums(v1, v2)
        thr = _topk_desc(cand)[PEER_TOPK - 1:PEER_TOPK, :]
        top = v1[0:1, :] + v2[0:1, :]
        zsum = jnp.sum(jnp.where(cand >= thr, jnp.exp(cand - top), 0.0), axis=0, keepdims=True)
        s1_ref[h] = s1
        e1_ref[h] = jnp.exp(s1 - v1[0:1, :]) / zsum
        s2_ref[h] = s2
        e2_ref[h] = jnp.exp(s2 - v2[0:1, :])
        thr_ref[pl.ds(h, 1), :] = thr
        return carry

    lax.fori_loop(0, PEER_HEADS, head, 0)


def _router(q, keys_1, keys_2):
    s = q.shape[0]
    tm = ROUTER_TM
    h_major = pl.BlockSpec((PEER_HEADS, PEER_KEYS, tm), lambda i: (0, 0, i))
    kspec = pl.BlockSpec((PEER_KEYS, PEER_HALF), lambda i: (0, 0))
    return pl.pallas_call(
        _router_kernel,
        grid=(s // tm,),
        in_specs=[pl.BlockSpec((tm, 2 * PEER_HALF * PEER_HEADS), lambda i: (i, 0)), kspec, kspec],
        out_specs=[h_major] * 4 + [pl.BlockSpec((PEER_HEADS, tm), lambda i: (0, i))],
        out_shape=[jax.ShapeDtypeStruct((PEER_HEADS, PEER_KEYS, s), jnp.float32)] * 4
        + [jax.ShapeDtypeStruct((PEER_HEADS, s), jnp.float32)],
        compiler_params=_params(_ARB1),
        name="router",
    )(q, keys_1, keys_2)


PEER_TM = 512
PEER_EC = 512
_A_PER_STEP = PEER_EC // PEER_KEYS


def _gelu(x):
    return 0.5 * x * (1.0 + lax.erf(x * (2.0 ** -0.5)))


_PEER_CHUNKS = PEER_EXPERTS // PEER_EC
_MXU_N = 256
_ACT_ROWS = 32
_SCORE_K = 256
_VALUE_K = 256


def _peer_kernel(h_ref, u_ref, v_ref, s1_ref, e1_ref, s2_ref, e2_ref, thr_ref, o_ref, sc0, sc1, act0, act1):
    step = pl.program_id(0)
    sc_ref, act_ref = (sc0, sc1), (act0, act1)

    @pl.when(step == 0)
    def _():
        sc0[...] = jnp.zeros(sc0.shape, sc0.dtype)
        act0[...] = jnp.zeros(act0.shape, act0.dtype)

    @pl.when(step > 0)
    def _():
        sc0[...] = sc1[...]
        act0[...] = act1[...]

    @pl.when(jnp.maximum(step - 2, 0) % _PEER_CHUNKS == 0)
    def _():
        o_ref[...] = jnp.zeros(o_ref.shape, jnp.float32)

    tm, d_model = h_ref.shape

    def stages(cur, nxt):
        def value_sum(nt, kt):
            dcol = slice(nt * _MXU_N, (nt + 1) * _MXU_N)
            ex = slice(kt * _VALUE_K, (kt + 1) * _VALUE_K)
            o_ref[:, dcol] += _dot_tn(act_ref[cur][ex, :], v_ref[ex, dcol])

        def activations(al, lt, bs):
            tok = slice(lt * LANES, (lt + 1) * LANES)
            keys2 = slice(bs * _ACT_ROWS, (bs + 1) * _ACT_ROWS)
            gate = jnp.zeros((_ACT_ROWS, LANES), jnp.float32)
            for h in range(PEER_HEADS):
                cand = s1_ref[al, h:h + 1, tok] + s2_ref[h, keys2, tok]
                w = e1_ref[al, h:h + 1, tok] * e2_ref[h, keys2, tok]
                gate = gate + jnp.where(cand >= thr_ref[h:h + 1, tok], w, 0.0)
            ex = slice(al * PEER_KEYS + bs * _ACT_ROWS, al * PEER_KEYS + (bs + 1) * _ACT_ROWS)
            act_ref[nxt][ex, tok] = (gate * _gelu(sc_ref[cur][ex, tok])).astype(act_ref[nxt].dtype)

        def scores(nt, kt):
            tok = slice(nt * _MXU_N, (nt + 1) * _MXU_N)
            ks = slice(kt * _SCORE_K, (kt + 1) * _SCORE_K)
            part = _dot_nt(u_ref[:, ks], h_ref[tok, ks])
            if kt == 0:
                sc_ref[nxt][:, tok] = part
            else:
                sc_ref[nxt][:, tok] += part

        s3 = [functools.partial(value_sum, nt, kt) for nt in range(d_model // _MXU_N)
              for kt in range(PEER_EC // _VALUE_K)]
        s2 = [functools.partial(activations, al, lt, bs) for al in range(_A_PER_STEP)
              for lt in range(tm // LANES) for bs in range(PEER_KEYS // _ACT_ROWS)]
        s1 = [functools.partial(scores, nt, kt) for nt in range(tm // _MXU_N) for kt in range(d_model // _SCORE_K)]
        rounds = max(len(s3), len(s2))
        for k in range(rounds):
            for pieces in (s3, s2, s1):
                lo, hi = k * len(pieces) // rounds, (k + 1) * len(pieces) // rounds
                for piece in pieces[lo:hi]:
                    piece()

    stages(0, 1)


def _peer(h2, u16, v16, s1, e1, s2, e2, thr):
    s, d = h2.shape
    tm, ec = PEER_TM, PEER_EC
    npairs = (s // tm) * _PEER_CHUNKS

    def pair(step, lag):
        idx = jnp.clip(step - lag, 0, npairs - 1)
        return idx // _PEER_CHUNKS, idx % _PEER_CHUNKS

    def tile(lag):
        return lambda t: (pair(t, lag)[0], 0)

    def chunk(lag):
        return lambda t: (pair(t, lag)[1], 0)

    a_major = pl.BlockSpec((_A_PER_STEP, PEER_HEADS, tm), lambda t: (pair(t, 1)[1], 0, pair(t, 1)[0]))
    h_major = pl.BlockSpec((PEER_HEADS, PEER_KEYS, tm), lambda t: (0, 0, pair(t, 1)[0]))
    return pl.pallas_call(
        _peer_kernel,
        grid=(npairs + 2,),
        in_specs=[pl.BlockSpec((tm, d), tile(0)),
                  pl.BlockSpec((ec, d), chunk(0)),
                  pl.BlockSpec((ec, d), chunk(2)),
                  a_major, a_major, h_major, h_major,
                  pl.BlockSpec((PEER_HEADS, tm), lambda t: (0, pair(t, 1)[0]))],
        out_specs=pl.BlockSpec((tm, d), tile(2)),
        out_shape=jax.ShapeDtypeStruct((s, d), jnp.float32),
        scratch_shapes=[pltpu.VMEM((ec, tm), jnp.float32)] * 2 + [pltpu.VMEM((ec, tm), jnp.bfloat16)] * 2,
        compiler_params=_params(_ARB1),
        name="peer",
    )(h2, u16, v16, s1, e1, s2, e2, thr)


def _rope_tables(positions):
    half = HEAD_DIM // 2
    inv_freq = ROPE_THETA ** (-jnp.arange(half, dtype=jnp.float32) / half)
    ang = positions.astype(jnp.float32)[:, None] * inv_freq[None, :]
    cos, sin = jnp.cos(ang), jnp.sin(ang)
    return jnp.concatenate([cos, cos], axis=-1), jnp.concatenate([-sin, sin], axis=-1)


def _layer(x, c_col, positions, w_ada, b_ada, attn_pre_norm, attn_post_norm, w_in, conv_w, a_log, dt_bias,
           dn_norm_w, w_o_attn, w_o_dn, w_out, ffn_pre_norm, ffn_post_norm, w_peer_q, peer_keys_1,
           peer_keys_2, expert_u, expert_v):
    d = D_MODEL
    bf = jnp.bfloat16
    mod = _adaln(c_col, w_ada, b_ada[None, :])
    shift_mix, scale_mix, gate_mix, shift_ffn, scale_ffn, gate_ffn = [mod[:, i * d:(i + 1) * d] for i in range(6)]

    h = _prenorm(x, attn_pre_norm[None, :], shift_mix, scale_mix)
    w_in16 = w_in.astype(bf)
    p_attn = _matmul_cols(h, w_in16, W_ATTN, W_DN - W_ATTN, jnp.float32, 1024, 512, "in_proj_attn")
    p_dn = _matmul_cols(h, w_in16, W_DN, W_SMALL - W_DN, bf, 1024, 512, "in_proj_dn")
    p_gates = _matmul(h, w_in16[:, W_GATES:], bf, 1024, 1024, "in_proj_gates")
    w_small = jnp.pad(w_in16[:, W_SMALL:W_GATES], ((0, 0), (0, LANES - 2 * DN_HEADS)))
    small_t = _small_proj(h, w_small)

    cos, sin = _rope_tables(positions)
    outs, lses = [], []
    for g, (_, dilation) in enumerate(DILATED_GROUPS):
        o, lse = _attention_group(p_attn, cos, sin, g, dilation)
        outs.append(o)
        lses.append(lse)

    qkvn = _dn_prep(p_dn, conv_w)
    y_d = _delta(qkvn, p_dn, small_t, a_log, dt_bias, dn_norm_w[None, :])

    merged = _merge(outs, lses, y_d, w_o_attn.astype(bf), w_o_dn.astype(bf), p_gates)
    mix = _matmul(merged, w_out.astype(bf), jnp.float32, 1024, 1024, "out_proj")
    x1, h2 = _post_mix(mix, x, gate_mix, attn_post_norm[None, :], ffn_pre_norm[None, :], shift_ffn, scale_ffn)

    q = _matmul(h2, w_peer_q.astype(bf), jnp.float32, 1024, 1024, "peer_q")
    s1, e1, s2, e2, thr = _router(q, peer_keys_1, peer_keys_2)
    s1, e1 = jnp.transpose(s1, (1, 0, 2)), jnp.transpose(e1, (1, 0, 2))
    y = _peer(h2, expert_u.astype(bf), expert_v.astype(bf), s1, e1, s2, e2, thr)
    return _final(y, x1, gate_ffn, ffn_post_norm[None, :])


def kernel(x, c, positions, w_ada, b_ada, attn_pre_norm, attn_post_norm, w_in, conv_w, a_log, dt_bias, dn_norm_w,
           w_o_attn, w_o_dn, w_out, ffn_pre_norm, ffn_post_norm, w_peer_q, peer_keys_1, peer_keys_2, expert_u,
           expert_v):
    batch, seq, d = x.shape
    depth = w_ada.shape[0]
    outs = []
    for b in range(batch):
        xb = x[b]
        c_col = c[b][:, None]
        for l in range(depth):
            xb = _layer(xb, c_col, positions[b], w_ada[l], b_ada[l], attn_pre_norm[l], attn_post_norm[l], w_in[l],
                        conv_w[l], a_log[l], dt_bias[l], dn_norm_w[l], w_o_attn[l], w_o_dn[l], w_out[l],
                        ffn_pre_norm[l], ffn_post_norm[l], w_peer_q[l], peer_keys_1[l], peer_keys_2[l],
                        expert_u[l], expert_v[l])
        outs.append(xb)
    return jnp.stack(outs, axis=0)
```
